```python
import jax, jax.numpy as jnp
from jax import lax
import numpy as np

D_MODEL = 1024
BATCH = 4
SEQ = 4096
DEPTH = 1
DEC_BATCH = 32
DEC_SEQ = 1
PAST_LEN = 8192
PAGE_SIZE = 128

N_META = 16
D_MIX = D_MODEL
D_ATTN = D_MIX // 2
H_A = 8
DH = D_ATTN // H_A
D_POOL = D_MIX - D_ATTN
POOL_WINDOWS = (2, 4, 8, 16)
POOL_GROUPS = len(POOL_WINDOWS)
POOL_CH = D_POOL // POOL_GROUPS
POOL_STATE = max(POOL_WINDOWS) - 1
D_FF = 2816
CONV_W = 3
Q_BLOCK = 128
EPS = 1e-6
ATTN_SCALE = DH ** -0.5
SB_BIAS_INIT = -7.0

kernel_name = "hymba_stickbreak_pool_convffn_step"


def rmsnorm(x, g):
    xf = x.astype(jnp.float32)
    r = lax.rsqrt(jnp.mean(xf * xf, axis=-1, keepdims=True) + EPS)
    return (xf * r * g.astype(jnp.float32)).astype(x.dtype)


def stick_breaking(q, k, v, q_pos, k_pos, sb_bias):
    z = jnp.einsum('bqhd,bkhd->bhqk', q.astype(jnp.float32), k.astype(jnp.float32)) * ATTN_SCALE
    z = z + sb_bias.astype(jnp.float32)[None, :, None, None]
    valid = k_pos[None, :] < q_pos[:, None]
    neg = jnp.where(valid, -jax.nn.softplus(z), 0.0)
    suffix = lax.cumsum(neg, axis=3, reverse=True) - neg
    a = jnp.where(valid, jnp.exp(jax.nn.log_sigmoid(z) + suffix), 0.0)
    o = jnp.einsum('bhqk,bkhd->bqhd', a, v.astype(jnp.float32))
    return o.astype(q.dtype)


def multiscale_pool(u, p0, n_keep, pool_w, pool_scale):
    B, L, _ = u.shape
    uf = u.astype(jnp.float32).reshape(B, L, POOL_GROUPS, POOL_CH)
    csum = jnp.cumsum(uf, axis=1)
    pos = p0 + jnp.arange(L)
    diffs = []
    for g, w in enumerate(POOL_WINDOWS):
        c = csum[:, :, g]
        lagged = jnp.pad(c, ((0, 0), (w, 0), (0, 0)))[:, :L]
        cnt = jnp.minimum(w, pos + 1).astype(jnp.float32)
        diffs.append((c - lagged) / cnt[None, :, None] - uf[:, :, g])
    d = jnp.stack(diffs, axis=2)[:, L - n_keep:]
    out = jnp.einsum('blgc,gce->blge', d, pool_w.astype(jnp.float32)) * pool_scale.astype(jnp.float32)
    return out.reshape(B, n_keep, D_POOL).astype(u.dtype)


def project_mix(h, w_in):
    B, L, _ = h.shape
    p = h @ w_in
    q = p[..., :D_ATTN].reshape(B, L, H_A, DH)
    k = p[..., D_ATTN:2 * D_ATTN].reshape(B, L, H_A, DH)
    v = p[..., 2 * D_ATTN:3 * D_ATTN].reshape(B, L, H_A, DH)
    u = p[..., 3 * D_ATTN:]
    return q, k, v, u


def conv_ffn(h, prefix, w_up, conv_w, conv_b, w_down):
    up = h @ w_up
    L = up.shape[1]
    ext = jnp.concatenate([prefix.astype(up.dtype), up], axis=1)
    c = conv_b + sum(conv_w[i] * ext[:, i:i + L] for i in range(CONV_W))
    gate, val = c[..., :D_FF], c[..., D_FF:]
    out = (jax.nn.silu(gate) * val) @ w_down
    return out, ext[:, L + 2 - (CONV_W - 1):]


def setup_inputs(seed: int = 0) -> dict:
    key = jax.random.key(seed)
    ks = jax.random.split(key, 24)
    n_pages = PAST_LEN // PAGE_SIZE
    n_used = DEC_BATCH * n_pages
    n_pool = n_used + n_used // 4
    f32 = jnp.float32
    nrm = lambda k, shape, s: jax.random.normal(k, shape, f32) * s
    perm = jax.random.permutation(ks[0], n_pool)[:n_used]
    page_table = perm.reshape(DEC_BATCH, n_pages).astype(jnp.int32)
    return {
        "x_prompt": nrm(ks[1], (BATCH, SEQ, D_MODEL), 1.0),
        "x_sample": nrm(ks[2], (DEC_BATCH, DEC_SEQ, D_MODEL), 1.0),
        "cache_k": nrm(ks[3], (n_pool, PAGE_SIZE, H_A, DH), 1.0),
        "cache_v": nrm(ks[4], (n_pool, PAGE_SIZE, H_A, DH), 1.0),
        "state_pool": nrm(ks[5], (DEC_BATCH, POOL_STATE, D_POOL), 1.0),
        "state_conv": nrm(ks[6], (DEC_BATCH, CONV_W - 1, 2 * D_FF), 1.0),
        "page_table": page_table,
        "meta_tokens": nrm(ks[7], (N_META, D_MODEL), 1.0),
        "norm_mix_g": 1.0 + nrm(ks[8], (D_MODEL,), 0.05),
        "w_in": nrm(ks[9], (D_MODEL, 3 * D_ATTN + D_POOL), D_MODEL ** -0.5),
        "sb_bias": SB_BIAS_INIT + nrm(ks[19], (H_A,), 0.1),
        "pool_w": nrm(ks[10], (POOL_GROUPS, POOL_CH, POOL_CH), POOL_CH ** -0.5),
        "pool_scale": 1.0 + nrm(ks[11], (POOL_GROUPS, POOL_CH), 0.1),
        "w_out": nrm(ks[12], (D_MIX, D_MODEL), D_MIX ** -0.5),
        "norm_ffn_g": 1.0 + nrm(ks[13], (D_MODEL,), 0.05),
        "w_up": nrm(ks[14], (D_MODEL, 2 * D_FF), D_MODEL ** -0.5),
        "conv_w": nrm(ks[15], (CONV_W, 2 * D_FF), CONV_W ** -0.5),
        "conv_b": nrm(ks[16], (2 * D_FF,), 0.02),
        "w_down": nrm(ks[17], (D_FF, D_MODEL), D_FF ** -0.5),
        "norm_final_g": 1.0 + nrm(ks[18], (D_MODEL,), 0.05),
    }


def reference(x_prompt, x_sample, cache_k, cache_v, state_pool, state_conv, page_table,
              meta_tokens, norm_mix_g, w_in, sb_bias, pool_w, pool_scale, w_out, norm_ffn_g,
              w_up, conv_w, conv_b, w_down, norm_final_g):
    B = x_prompt.shape[0]
    T = N_META + SEQ
    n_blocks = SEQ // Q_BLOCK

    meta = jnp.broadcast_to(meta_tokens.astype(x_prompt.dtype)[None], (B, N_META, D_MODEL))
    xp = jnp.concatenate([meta, x_prompt], axis=1)
    for _ in range(DEPTH):
        h = rmsnorm(xp, norm_mix_g)
        q, k, v, u = project_mix(h, w_in)
        k_prompt, v_prompt = k, v
        k_pos = jnp.arange(T)
        o_meta = stick_breaking(q[:, :N_META], k[:, :N_META], v[:, :N_META],
                                jnp.arange(N_META), jnp.arange(N_META), sb_bias)
        q_blk = q[:, N_META:].reshape(B, n_blocks, Q_BLOCK, H_A, DH).transpose(1, 0, 2, 3, 4)
        qpos_blk = (N_META + jnp.arange(SEQ)).reshape(n_blocks, Q_BLOCK)
        o_real = lax.map(lambda a: stick_breaking(a[0], k, v, a[1], k_pos, sb_bias), (q_blk, qpos_blk))
        o_real = o_real.transpose(1, 0, 2, 3, 4).reshape(B, SEQ, H_A, DH)
        o_attn = jnp.concatenate([o_meta, o_real], axis=1).reshape(B, T, D_ATTN)
        o_pool = multiscale_pool(u, 0, T, pool_w, pool_scale)
        pool_prompt = u[:, T - POOL_STATE:]
        xp = xp + jnp.concatenate([o_attn, o_pool], axis=-1) @ w_out
        h2 = rmsnorm(xp, norm_ffn_g)
        zero_prefix = jnp.zeros((B, CONV_W - 1, 2 * D_FF), xp.dtype)
        f, conv_prompt = conv_ffn(h2, zero_prefix, w_up, conv_w, conv_b, w_down)
        xp = xp + f
    y_prompt = rmsnorm(xp, norm_final_g)[:, N_META:]

    DB, L = x_sample.shape[0], x_sample.shape[1]
    xs = x_sample
    for _ in range(DEPTH):
        h = rmsnorm(xs, norm_mix_g)
        q, k_new, v_new, u_new = project_mix(h, w_in)
        k_sample, v_sample = k_new, v_new
        k_past = cache_k[page_table].reshape(DB, PAST_LEN, H_A, DH)
        v_past = cache_v[page_table].reshape(DB, PAST_LEN, H_A, DH)
        k_all = jnp.concatenate([k_past.astype(k_new.dtype), k_new], axis=1)
        v_all = jnp.concatenate([v_past.astype(v_new.dtype), v_new], axis=1)
        o_attn = stick_breaking(q, k_all, v_all, PAST_LEN + jnp.arange(L),
                                jnp.arange(PAST_LEN + L), sb_bias).reshape(DB, L, D_ATTN)
        u_ext = jnp.concatenate([state_pool.astype(u_new.dtype), u_new], axis=1)
        o_pool = multiscale_pool(u_ext, PAST_LEN - POOL_STATE, L, pool_w, pool_scale)
        pool_sample = u_ext[:, u_ext.shape[1] - POOL_STATE:]
        xs = xs + jnp.concatenate([o_attn, o_pool], axis=-1) @ w_out
        h2 = rmsnorm(xs, norm_ffn_g)
        f, conv_sample = conv_ffn(h2, state_conv, w_up, conv_w, conv_b, w_down)
        xs = xs + f
    y_sample = rmsnorm(xs, norm_final_g)

    return (y_prompt, y_sample, k_prompt, v_prompt, pool_prompt, conv_prompt,
            k_sample, v_sample, pool_sample, conv_sample)
```

```python
import functools

import jax
import jax.numpy as jnp
from jax import lax
from jax.experimental import pallas as pl
from jax.experimental.pallas import tpu as pltpu

F32 = jnp.float32
BF16 = jnp.bfloat16

N_META = 16
H_A = 8
POOL_WINDOWS = (2, 4, 8, 16)
CONV_W = 3
EPS = 1e-6

LANES = 128
SUBLANES = 8
VMEM_LIMIT = 56 * 1024 * 1024

POOL_HALO = 16
CONV_HALO = SUBLANES

PROJ_ROWS = 512
ATTN_TILE = 256
FFN_ROWS = 256
FFN_CHUNK = 256
PAGES_PER_STEP = 8


def _rmsnorm(x, g):
    r = lax.rsqrt(jnp.mean(x * x, axis=-1, keepdims=True) + EPS)
    return x * r * g


def _dot(a, b):
    return jnp.dot(a, b, preferred_element_type=F32)


def _dot_nt(a, b):
    return lax.dot_general(a, b, (((1,), (1,)), ((), ())), preferred_element_type=F32)


def _strict_lower_ones(n):
    j = lax.broadcasted_iota(jnp.int32, (n, n), 0)
    s = lax.broadcasted_iota(jnp.int32, (n, n), 1)
    return (j > s).astype(BF16)


def _softplus_terms(z):
    softplus = jnp.maximum(z, 0.0) + jnp.log(1.0 + jnp.exp(-jnp.abs(z)))
    return softplus, z - softplus


def _stick_block(z, vb, mask, tri, later):
    softplus, log_beta = _softplus_terms(z)
    if mask is not None:
        softplus = jnp.where(mask, softplus, 0.0)
    hi = softplus.astype(BF16)
    lo = (softplus - hi.astype(F32)).astype(BF16)
    suffix = _dot(hi, tri) + _dot(lo, tri)
    a = jnp.exp(log_beta - suffix - later)
    if mask is not None:
        a = jnp.where(mask, a, 0.0)
    pv = _dot(a.astype(BF16), vb)
    return pv, later + jnp.sum(softplus, axis=1, keepdims=True)


def _proj_kernel(x_ref, g_ref, w_ref, q_ref, k_ref, v_ref, u_ref, kb_ref, vb_ref, *, scale):
    d_attn = q_ref.shape[-1]
    h = _rmsnorm(x_ref[...], g_ref[...]).astype(BF16)
    p = _dot(h, w_ref[...])
    k = p[:, d_attn:2 * d_attn]
    v = p[:, 2 * d_attn:3 * d_attn]
    q_ref[...] = (p[:, :d_attn] * scale).astype(BF16)
    k_ref[...] = k
    v_ref[...] = v
    u_ref[...] = p[:, 3 * d_attn:]
    kb_ref[...] = k.astype(BF16)
    vb_ref[...] = v.astype(BF16)


def _proj(x, g, w_in, d_attn, tm):
    rows, d = x.shape
    n = w_in.shape[1]
    d_pool = n - 3 * d_attn
    dh = d_attn // H_A
    row_spec = lambda width: pl.BlockSpec((tm, width), lambda i: (i, 0))
    return pl.pallas_call(
        functools.partial(_proj_kernel, scale=dh ** -0.5),
        grid=(rows // tm,),
        in_specs=[row_spec(d),
                  pl.BlockSpec((1, d), lambda i: (0, 0)),
                  pl.BlockSpec((d, n), lambda i: (0, 0))],
        out_specs=[row_spec(d_attn), row_spec(d_attn), row_spec(d_attn), row_spec(d_pool),
                   row_spec(d_attn), row_spec(d_attn)],
        out_shape=[jax.ShapeDtypeStruct((rows, d_attn), BF16),
                   jax.ShapeDtypeStruct((rows, d_attn), F32),
                   jax.ShapeDtypeStruct((rows, d_attn), F32),
                   jax.ShapeDtypeStruct((rows, d_pool), F32),
                   jax.ShapeDtypeStruct((rows, d_attn), BF16),
                   jax.ShapeDtypeStruct((rows, d_attn), BF16)],
        compiler_params=pltpu.CompilerParams(dimension_semantics=("arbitrary",),
                                             vmem_limit_bytes=VMEM_LIMIT),
        name="proj",
    )(x, g, w_in)


def _attn_kernel(bias_ref, q_ref, k_ref, v_ref, km_ref, vm_ref, tri_ref, o_ref, acc_ref, later_ref):
    j = pl.program_id(1)
    i = pl.program_id(2)
    tq = q_ref.shape[1]
    tk = tq
    dh = LANES // 2
    q = q_ref[0]
    lane = lax.broadcasted_iota(jnp.int32, (tq, LANES), 1)
    row = lax.broadcasted_iota(jnp.int32, (tq, tk), 0)
    col = lax.broadcasted_iota(jnp.int32, (tq, tk), 1)
    diag_mask = col < row
    n_meta_pad = km_ref.shape[0]
    meta_mask = lax.broadcasted_iota(jnp.int32, (tq, n_meta_pad), 1) < N_META
    tri = tri_ref[...]
    tri_meta = tri_ref[0:n_meta_pad, 0:n_meta_pad]
    diag_start = pl.multiple_of(i * tq, tq)

    out = None
    for p in range(2):
        bias = bias_ref[2 * j + p]
        in_head = (lane >= dh) if p else (lane < dh)
        qm = jnp.where(in_head, q, jnp.zeros_like(q))

        z = _dot_nt(qm, k_ref[0, pl.ds(diag_start, tk), :]) + bias
        pv, later = _stick_block(z, v_ref[0, pl.ds(diag_start, tk), :], diag_mask, tri, 0.0)
        acc_ref[...] = pv
        later_ref[...] = later

        def body(n, carry, qm=qm, bias=bias):
            start = pl.multiple_of((i - 1 - n) * tk, tk)
            z = _dot_nt(qm, k_ref[0, pl.ds(start, tk), :]) + bias
            pv, later = _stick_block(z, v_ref[0, pl.ds(start, tk), :], None, tri, later_ref[...])
            acc_ref[...] += pv
            later_ref[...] = later
            return carry

        lax.fori_loop(0, i, body, 0)

        z = _dot_nt(qm, km_ref[...]) + bias
        pv, _ = _stick_block(z, vm_ref[...], meta_mask, tri_meta, later_ref[...])
        res = acc_ref[...] + pv
        out = res if p == 0 else jnp.where(lane < dh, out, res)
    o_ref[0] = out.astype(o_ref.dtype)


def _attn_prompt(sb_bias, q, kb, vb, km, vm, tq):
    b, s, d_attn = q.shape
    tri = _strict_lower_ones(tq)
    n_meta_pad = km.shape[0]
    return pl.pallas_call(
        _attn_kernel,
        grid=(b, d_attn // LANES, s // tq),
        in_specs=[pl.BlockSpec(memory_space=pltpu.SMEM),
                  pl.BlockSpec((1, tq, LANES), lambda b, j, i: (b, i, j)),
                  pl.BlockSpec((1, s, LANES), lambda b, j, i: (b, 0, j)),
                  pl.BlockSpec((1, s, LANES), lambda b, j, i: (b, 0, j)),
                  pl.BlockSpec((n_meta_pad, LANES), lambda b, j, i: (0, j)),
                  pl.BlockSpec((n_meta_pad, LANES), lambda b, j, i: (0, j)),
                  pl.BlockSpec((tq, tq), lambda b, j, i: (0, 0))],
        out_specs=pl.BlockSpec((1, tq, LANES), lambda b, j, i: (b, i, j)),
        out_shape=jax.ShapeDtypeStruct((b, s, d_attn), BF16),
        scratch_shapes=[pltpu.VMEM((tq, LANES), F32), pltpu.VMEM((tq, 1), F32)],
        compiler_params=pltpu.CompilerParams(
            dimension_semantics=("arbitrary", "arbitrary", "arbitrary"),
            vmem_limit_bytes=VMEM_LIMIT),
        name="attn_prompt",
    )(sb_bias, q, kb, vb, km, vm, tri)


def _attn_sample_kernel(pt_ref, q_ref, knew_ref, vnew_ref, bias_ref, tri_ref, *refs, past_len):
    del pt_ref
    npg = (len(refs) - 3) // 2
    k_refs, v_refs = refs[:npg], refs[npg:2 * npg]
    o_ref, acc_ref, later_ref = refs[2 * npg:]
    g = pl.program_id(1)
    d_attn = q_ref.shape[-1]
    dh = d_attn // H_A
    n_new = knew_ref.shape[1]
    head = lax.broadcasted_iota(jnp.int32, (H_A, d_attn), 0)
    lane = lax.broadcasted_iota(jnp.int32, (H_A, d_attn), 1)
    own = (lane >= head * dh) & (lane < (head + 1) * dh)
    qmat_f32 = jnp.where(own, q_ref[0].astype(F32), 0.0)
    qmat = qmat_f32.astype(BF16)
    bias = bias_ref[...]
    tri = tri_ref[...]

    @pl.when(g == 0)
    def _():
        as_dot_operand = lambda a: a.astype(BF16).astype(F32)
        k_pos = past_len + lax.broadcasted_iota(jnp.int32, (H_A, n_new), 1)
        q_pos = past_len + n_new - 1
        visible = k_pos < q_pos
        z = jnp.sum(qmat_f32 * as_dot_operand(knew_ref[0]), axis=1, keepdims=True) + bias
        softplus, log_beta = _softplus_terms(z)
        a = jnp.where(visible, jnp.exp(log_beta), 0.0)
        acc_ref[...] = as_dot_operand(a) * as_dot_operand(vnew_ref[0])
        later_ref[...] = jnp.where(visible, softplus, 0.0)

    for r in reversed(range(npg)):
        z = _dot_nt(qmat, k_refs[r][0].astype(BF16)) + bias
        pv, later = _stick_block(z, v_refs[r][0].astype(BF16), None, tri, later_ref[...])
        acc_ref[...] += pv
        later_ref[...] = later

    @pl.when(g == pl.num_programs(1) - 1)
    def _():
        o_ref[0] = jnp.sum(jnp.where(own, acc_ref[...], 0.0), axis=0, keepdims=True).astype(o_ref.dtype)


def _attn_sample(sb_bias, q, k_new, v_new, cache_k, cache_v, page_table):
    db, n_new, d_attn = q.shape
    n_pool, page = cache_k.shape[0], cache_k.shape[1]
    n_pages = page_table.shape[1]
    npg = PAGES_PER_STEP
    n_steps = n_pages // npg
    ck = cache_k.reshape(n_pool, page, d_attn)
    cv = cache_v.reshape(n_pool, page, d_attn)
    tri = _strict_lower_ones(page)
    bias = sb_bias.reshape(H_A, 1)

    def page_spec(r):
        return pl.BlockSpec((1, page, d_attn),
                            lambda b, g, pt: (pt[b, (n_steps - 1 - g) * npg + r], 0, 0))

    tok_spec = pl.BlockSpec((1, n_new, d_attn), lambda b, g, pt: (b, 0, 0))
    grid_spec = pltpu.PrefetchScalarGridSpec(
        num_scalar_prefetch=1,
        grid=(db, n_steps),
        in_specs=[tok_spec, tok_spec, tok_spec,
                  pl.BlockSpec((H_A, 1), lambda b, g, pt: (0, 0)),
                  pl.BlockSpec((page, page), lambda b, g, pt: (0, 0))]
                 + [page_spec(r) for r in range(npg)] + [page_spec(r) for r in range(npg)],
        out_specs=tok_spec,
        scratch_shapes=[pltpu.VMEM((H_A, d_attn), F32), pltpu.VMEM((H_A, 1), F32)],
    )
    return pl.pallas_call(
        functools.partial(_attn_sample_kernel, past_len=n_pages * page),
        grid_spec=grid_spec,
        out_shape=jax.ShapeDtypeStruct((db, n_new, d_attn), BF16),
        compiler_params=pltpu.CompilerParams(dimension_semantics=("arbitrary", "arbitrary"),
                                             vmem_limit_bytes=VMEM_LIMIT),
        name="attn_sample",
    )(page_table, q, k_new, v_new, bias, tri, *([ck] * npg), *([cv] * npg))


def _pool_heads(window_sums, cur, inv_cnt, pw_ref, ps_ref):
    outs = []
    for g in range(len(POOL_WINDOWS)):
        d = window_sums[g] * inv_cnt[g] - cur[g]
        outs.append(_dot(d.astype(BF16), pw_ref[g]) * ps_ref[g:g + 1, :])
    return jnp.concatenate(outs, axis=1)


def _pool_tile(uext_ref, tm, pos0, pw_ref, ps_ref):
    ch = pw_ref.shape[-1]
    pos = pos0 + lax.broadcasted_iota(jnp.int32, (tm, ch), 0)
    sums, cur, inv = [], [], []
    for g, w in enumerate(POOL_WINDOWS):
        lanes = slice(g * ch, (g + 1) * ch)
        c = uext_ref[pl.ds(POOL_HALO, tm), lanes]
        s = c
        for back in range(1, w):
            s = s + uext_ref[pl.ds(POOL_HALO - back, tm), lanes]
        cur.append(c)
        sums.append(s)
        inv.append(1.0 / jnp.minimum(w, pos + 1).astype(F32))
    return _pool_heads(sums, cur, inv, pw_ref, ps_ref)


def _mix_residual(x, o_attn, o_pool, wout_ref):
    d_attn = o_attn.shape[-1]
    mix = _dot(o_attn, wout_ref[0:d_attn, :]) + _dot(o_pool.astype(BF16), wout_ref[d_attn:, :])
    return x + mix


def _conv3(cw_ref, cb_ref, cols, two_back, one_back, cur):
    return (cb_ref[:, cols] + cw_ref[0:1, cols] * two_back + cw_ref[1:2, cols] * one_back
            + cw_ref[2:3, cols] * cur)


def _mixffn_kernel(x_ref, oa_ref, u_ref, umeta_ref, upmeta_ref, pw_ref, ps_ref, wout_ref, g2_ref,
                   wup_ref, cw_ref, cb_ref, wdown_ref, g3_ref, y_ref, conv_ref,
                   uext_ref, carry_ref, ext_ref):
    i = pl.program_id(1)
    tm = x_ref.shape[1]
    d_ff = wdown_ref.shape[0]

    @pl.when(i == 0)
    def _():
        uext_ref[0:POOL_HALO, :] = umeta_ref[...]
        carry_ref[...] = upmeta_ref[...]

    @pl.when(i > 0)
    def _():
        uext_ref[0:POOL_HALO, :] = uext_ref[tm:tm + POOL_HALO, :]

    uext_ref[POOL_HALO:POOL_HALO + tm, :] = u_ref[0]
    o_pool = _pool_tile(uext_ref, tm, N_META + i * tm, pw_ref, ps_ref)
    xp = _mix_residual(x_ref[0], oa_ref[0], o_pool, wout_ref)
    h = _rmsnorm(xp, g2_ref[...]).astype(BF16)

    f = jnp.zeros_like(xp)
    for c in range(d_ff // FFN_CHUNK):
        halves = []
        for part in range(2):
            cols = slice(part * d_ff + c * FFN_CHUNK, part * d_ff + (c + 1) * FFN_CHUNK)
            up = _dot(h, wup_ref[:, cols])
            ext_ref[part, 0:CONV_HALO, :] = carry_ref[:, cols]
            ext_ref[part, CONV_HALO:CONV_HALO + tm, :] = up
            carry_ref[:, cols] = up[tm - CONV_HALO:, :]
            conv_ref[0, :, cols] = ext_ref[part, pl.ds(CONV_HALO + tm - (CONV_W - 1), CONV_W - 1), :]
            halves.append(_conv3(cw_ref, cb_ref, cols,
                                 ext_ref[part, pl.ds(CONV_HALO - 2, tm), :],
                                 ext_ref[part, pl.ds(CONV_HALO - 1, tm), :], up))
        gate, val = halves
        act = gate * jax.nn.sigmoid(gate) * val
        f = f + _dot(act.astype(BF16), wdown_ref[c * FFN_CHUNK:(c + 1) * FFN_CHUNK, :])
    y_ref[0] = _rmsnorm(xp + f, g3_ref[...])


def _mixffn_prompt(x, o_attn, u, u_meta, up_meta_tail, weights, tm):
    b, s, d = x.shape
    pool_w, pool_scale, w_out, g2, w_up, conv_w, conv_b, w_down, g3 = weights
    d_attn, d_pool, d_ff2 = o_attn.shape[-1], u.shape[-1], w_up.shape[1]
    const = lambda a: pl.BlockSpec(a.shape, lambda b, i: (0,) * a.ndim, pipeline_mode=pl.Buffered(1))
    row_spec = lambda width: pl.BlockSpec((1, tm, width), lambda b, i: (b, i, 0))
    return pl.pallas_call(
        _mixffn_kernel,
        grid=(b, s // tm),
        in_specs=[row_spec(d), row_spec(d_attn), row_spec(d_pool), const(u_meta), const(up_meta_tail),
                  const(pool_w), const(pool_scale), const(w_out), const(g2), const(w_up),
                  const(conv_w), const(conv_b), const(w_down), const(g3)],
        out_specs=[row_spec(d), pl.BlockSpec((1, CONV_W - 1, d_ff2), lambda b, i: (b, 0, 0))],
        out_shape=[jax.ShapeDtypeStruct((b, s, d), F32),
                   jax.ShapeDtypeStruct((b, CONV_W - 1, d_ff2), F32)],
        scratch_shapes=[pltpu.VMEM((POOL_HALO + tm, d_pool), F32),
                        pltpu.VMEM((CONV_HALO, d_ff2), F32),
                        pltpu.VMEM((2, CONV_HALO + tm, FFN_CHUNK), F32)],
        compiler_params=pltpu.CompilerParams(dimension_semantics=("arbitrary", "arbitrary"),
                                             vmem_limit_bytes=VMEM_LIMIT),
        name="mixffn_prompt",
    )(x, o_attn, u, u_meta, up_meta_tail, pool_w, pool_scale, w_out, g2, w_up, conv_w, conv_b, w_down, g3)


def _meta_kernel(bias_ref, x_ref, q_ref, kb_ref, vb_ref, u_ref, pw_ref, ps_ref, wout_ref, g2_ref,
                 wup_ref, up_ref, uext_ref):
    n, d_attn = q_ref.shape
    n_keys = kb_ref.shape[0]
    dh = LANES // 2
    lane = lax.broadcasted_iota(jnp.int32, (n, LANES), 1)
    row = lax.broadcasted_iota(jnp.int32, (n, n_keys), 0)
    col = lax.broadcasted_iota(jnp.int32, (n, n_keys), 1)
    causal = col < row
    tri = _strict_lower_ones(n_keys)
    pairs = []
    for j in range(d_attn // LANES):
        cols = slice(j * LANES, (j + 1) * LANES)
        q, kb, vb = q_ref[:, cols], kb_ref[:, cols], vb_ref[:, cols]
        out = None
        for p in range(2):
            in_head = (lane >= dh) if p else (lane < dh)
            qm = jnp.where(in_head, q, jnp.zeros_like(q))
            z = _dot_nt(qm, kb) + bias_ref[2 * j + p]
            pv, _ = _stick_block(z, vb, causal, tri, 0.0)
            out = pv if p == 0 else jnp.where(lane < dh, out, pv)
        pairs.append(out)
    o_attn = jnp.concatenate(pairs, axis=1).astype(BF16)

    uext_ref[0:POOL_HALO, :] = jnp.zeros((POOL_HALO, uext_ref.shape[1]), F32)
    uext_ref[POOL_HALO:POOL_HALO + n, :] = u_ref[...]
    o_pool = _pool_tile(uext_ref, n, 0, pw_ref, ps_ref)
    xp = _mix_residual(x_ref[...], o_attn, o_pool, wout_ref)
    h = _rmsnorm(xp, g2_ref[...]).astype(BF16)
    up_ref[...] = _dot(h, wup_ref[...])


def _meta_rows(sb_bias, x, q, kb, vb, u, weights):
    pool_w, pool_scale, w_out, g2, w_up = weights
    n = x.shape[0]
    vmem = pl.BlockSpec(memory_space=pltpu.VMEM)
    return pl.pallas_call(
        _meta_kernel,
        in_specs=[pl.BlockSpec(memory_space=pltpu.SMEM)] + [vmem] * 10,
        out_specs=vmem,
        out_shape=jax.ShapeDtypeStruct((n, w_up.shape[1]), F32),
        scratch_shapes=[pltpu.VMEM((POOL_HALO + n, u.shape[1]), F32)],
        compiler_params=pltpu.CompilerParams(vmem_limit_bytes=VMEM_LIMIT),
        name="meta_rows",
    )(sb_bias, x, q, kb, vb, u, pool_w, pool_scale, w_out, g2, w_up)


def _mixffn_sample_kernel(x_ref, oa_ref, u_ref, pstate_ref, cstate_ref, pw_ref, ps_ref, wout_ref,
                          g2_ref, wup_ref, cw_ref, cb_ref, wdown_ref, g3_ref, y_ref, up_ref, *, pos):
    n_state = pstate_ref.shape[0]
    ch = pw_ref.shape[-1]
    d_ff = wdown_ref.shape[0]
    sums, cur, inv = [], [], []
    for g, w in enumerate(POOL_WINDOWS):
        lanes = slice(g * ch, (g + 1) * ch)
        c = u_ref[:, lanes]
        s = c
        for back in range(1, w):
            s = s + pstate_ref[n_state - back, :, lanes]
        cur.append(c)
        sums.append(s)
        inv.append(1.0 / min(w, pos + 1))
    o_pool = _pool_heads(sums, cur, inv, pw_ref, ps_ref)
    xp = _mix_residual(x_ref[...], oa_ref[...], o_pool, wout_ref)
    h = _rmsnorm(xp, g2_ref[...]).astype(BF16)
    up = _dot(h, wup_ref[...])
    up_ref[...] = up
    c = _conv3(cw_ref, cb_ref, slice(None), cstate_ref[0], cstate_ref[1], up)
    gate, val = c[:, :d_ff], c[:, d_ff:]
    act = gate * jax.nn.sigmoid(gate) * val
    f = _dot(act.astype(BF16), wdown_ref[...])
    y_ref[...] = _rmsnorm(xp + f, g3_ref[...])


def _mixffn_sample(x, o_attn, u, pool_state_t, conv_state_t, weights, pos):
    pool_w, pool_scale, w_out, g2, w_up, conv_w, conv_b, w_down, g3 = weights
    n, d = x.shape
    vmem = pl.BlockSpec(memory_space=pltpu.VMEM)
    return pl.pallas_call(
        functools.partial(_mixffn_sample_kernel, pos=pos),
        in_specs=[vmem] * 14,
        out_specs=[vmem, vmem],
        out_shape=[jax.ShapeDtypeStruct((n, d), F32), jax.ShapeDtypeStruct((n, w_up.shape[1]), F32)],
        compiler_params=pltpu.CompilerParams(vmem_limit_bytes=VMEM_LIMIT),
        name="mixffn_sample",
    )(x, o_attn, u, pool_state_t, conv_state_t, pool_w, pool_scale, w_out, g2, w_up, conv_w, conv_b,
      w_down, g3)


def kernel(x_prompt, x_sample, cache_k, cache_v, state_pool, state_conv, page_table, meta_tokens,
           norm_mix_g, w_in, sb_bias, pool_w, pool_scale, w_out, norm_ffn_g, w_up, conv_w, conv_b,
           w_down, norm_final_g):
    b, s, d = x_prompt.shape
    db, n_new, _ = x_sample.shape
    assert n_new == 1, "the sample path handles one new token per sequence"
    dh = cache_k.shape[-1]
    d_attn = H_A * dh
    d_pool = state_pool.shape[-1]
    past_len = page_table.shape[1] * cache_k.shape[1]
    t = N_META + s

    g1 = norm_mix_g.reshape(1, d)
    g2 = norm_ffn_g.reshape(1, d)
    g3 = norm_final_g.reshape(1, d)
    w_in_b, w_out_b, w_up_b, w_down_b = (w.astype(BF16) for w in (w_in, w_out, w_up, w_down))
    pool_w_b = pool_w.astype(BF16)
    conv_b2 = conv_b.reshape(1, -1)
    sb_bias = sb_bias.astype(F32)
    mix_weights = (pool_w_b, pool_scale, w_out_b, g2, w_up_b)
    ffn_weights = mix_weights + (conv_w, conv_b2, w_down_b, g3)

    q_m, k_m, v_m, u_m, kb_m, vb_m = _proj(meta_tokens, g1, w_in_b, d_attn, N_META)
    pad_meta = lambda a: jnp.pad(a, ((0, LANES - N_META), (0, 0)))
    kb_m, vb_m = pad_meta(kb_m), pad_meta(vb_m)
    up_m = _meta_rows(sb_bias, meta_tokens, q_m, kb_m, vb_m, u_m, mix_weights)

    q_r, k_r, v_r, u_r, kb_r, vb_r = _proj(x_prompt.reshape(b * s, d), g1, w_in_b, d_attn, PROJ_ROWS)
    seq = lambda a: a.reshape(b, s, a.shape[-1])
    o_attn = _attn_prompt(sb_bias, seq(q_r), seq(kb_r), seq(vb_r), kb_m, vb_m, ATTN_TILE)
    y_prompt, conv_prompt = _mixffn_prompt(x_prompt, o_attn, seq(u_r), u_m, up_m[N_META - CONV_HALO:],
                                           ffn_weights, FFN_ROWS)
    with_meta = lambda m, r: jnp.concatenate(
        [jnp.broadcast_to(m[None], (b,) + m.shape), seq(r)], axis=1).reshape(b, t, H_A, dh)
    k_prompt = with_meta(k_m, k_r)
    v_prompt = with_meta(v_m, v_r)
    pool_prompt = seq(u_r)[:, s - state_pool.shape[1]:]

    q_s, k_s, v_s, u_s, _, _ = _proj(x_sample.reshape(db * n_new, d), g1, w_in_b, d_attn, db * n_new)
    tok = lambda a: a.reshape(db, n_new, a.shape[-1])
    o_attn_s = _attn_sample(sb_bias, tok(q_s), tok(k_s), tok(v_s), cache_k, cache_v, page_table)
    y_s, up_s = _mixffn_sample(x_sample.reshape(db, d), o_attn_s.reshape(db, d_attn), u_s,
                               jnp.swapaxes(state_pool, 0, 1), jnp.swapaxes(state_conv, 0, 1),
                               ffn_weights, past_len)
    y_sample = y_s.reshape(db, n_new, d)
    k_sample = k_s.reshape(db, n_new, H_A, dh)
    v_sample = v_s.reshape(db, n_new, H_A, dh)
    pool_sample = jnp.concatenate([state_pool[:, 1:], u_s.reshape(db, n_new, d_pool)], axis=1)
    conv_sample = jnp.concatenate([state_conv[:, 1:], up_s[:, None, :]], axis=1)

    return (y_prompt, y_sample, k_prompt, v_prompt, pool_prompt, conv_prompt,
            k_sample, v_sample, pool_sample, conv_sample)
```

```python
import functools

import jax
import jax.numpy as jnp
from jax import lax
from jax.experimental import pallas as pl
from jax.experimental.pallas import tpu as pltpu

F32 = jnp.float32
BF16 = jnp.bfloat16

N_META = 16
H_A = 8
POOL_WINDOWS = (2, 4, 8, 16)
CONV_W = 3
EPS = 1e-6
LOG2_E = 1.4426950408889634

LANES = 128
SUBLANES = 8
VMEM_LIMIT = 56 * 1024 * 1024

POOL_HALO = 16
CONV_HALO = SUBLANES

PROJ_ROWS = 512
ATTN_Q_TILE = 512
ATTN_K_BLOCK = 256
FFN_ROWS = 256
FFN_CHUNK = 256
PAGES_PER_STEP = 8


def _rmsnorm(x, g):
    r = lax.rsqrt(jnp.mean(x * x, axis=-1, keepdims=True) + EPS)
    return x * r * g


def _dot(a, b):
    return jnp.dot(a, b, preferred_element_type=F32)


def _dot_nt(a, b):
    return lax.dot_general(a, b, (((1,), (1,)), ((), ())), preferred_element_type=F32)


def _strict_lower_ones(n):
    j = lax.broadcasted_iota(jnp.int32, (n, n), 0)
    s = lax.broadcasted_iota(jnp.int32, (n, n), 1)
    return (j > s).astype(BF16)


def _softplus_terms(z):
    softplus = jnp.maximum(z, 0.0) + jnp.log2(1.0 + jnp.exp2(-jnp.abs(z)))
    return softplus, z - softplus


def _stick_block(z, vb, mask, tri, later):
    softplus, log_beta = _softplus_terms(z)
    if mask is not None:
        softplus = jnp.where(mask, softplus, 0.0)
    hi = softplus.astype(BF16)
    lo = (softplus - hi.astype(F32)).astype(BF16)
    suffix = _dot(hi, tri) + _dot(lo, tri)
    a = jnp.exp2(log_beta - suffix - later)
    if mask is not None:
        a = jnp.where(mask, a, 0.0)
    pv = _dot(a.astype(BF16), vb)
    return pv, later + jnp.sum(softplus, axis=1, keepdims=True)


def _proj_kernel(x_ref, g_ref, w_ref, q_ref, k_ref, v_ref, u_ref, kb_ref, vb_ref, *, scale):
    d_attn = q_ref.shape[-1]
    h = _rmsnorm(x_ref[...], g_ref[...]).astype(BF16)
    p = _dot(h, w_ref[...])
    k = p[:, d_attn:2 * d_attn]
    v = p[:, 2 * d_attn:3 * d_attn]
    q_ref[...] = (p[:, :d_attn] * scale).astype(BF16)
    k_ref[...] = k
    v_ref[...] = v
    u_ref[...] = p[:, 3 * d_attn:]
    kb_ref[...] = k.astype(BF16)
    vb_ref[...] = v.astype(BF16)


def _proj(x, g, w_in, d_attn, tm):
    rows, d = x.shape
    n = w_in.shape[1]
    d_pool = n - 3 * d_attn
    dh = d_attn // H_A
    row_spec = lambda width: pl.BlockSpec((tm, width), lambda i: (i, 0))
    return pl.pallas_call(
        functools.partial(_proj_kernel, scale=dh ** -0.5 * LOG2_E),
        grid=(rows // tm,),
        in_specs=[row_spec(d),
                  pl.BlockSpec((1, d), lambda i: (0, 0)),
                  pl.BlockSpec((d, n), lambda i: (0, 0))],
        out_specs=[row_spec(d_attn), row_spec(d_attn), row_spec(d_attn), row_spec(d_pool),
                   row_spec(d_attn), row_spec(d_attn)],
        out_shape=[jax.ShapeDtypeStruct((rows, d_attn), BF16),
                   jax.ShapeDtypeStruct((rows, d_attn), F32),
                   jax.ShapeDtypeStruct((rows, d_attn), F32),
                   jax.ShapeDtypeStruct((rows, d_pool), F32),
                   jax.ShapeDtypeStruct((rows, d_attn), BF16),
                   jax.ShapeDtypeStruct((rows, d_attn), BF16)],
        compiler_params=pltpu.CompilerParams(dimension_semantics=("arbitrary",),
                                             vmem_limit_bytes=VMEM_LIMIT),
        name="proj",
    )(x, g, w_in)


def _attn_kernel(bias_ref, q_ref, k_ref, v_ref, km_ref, vm_ref, tri_ref, o_ref, acc_ref, later_ref):
    j = pl.program_id(1)
    i = pl.program_id(2)
    tq = q_ref.shape[1]
    tk = tri_ref.shape[0]
    n_sub = tq // tk
    dh = LANES // 2
    q = q_ref[0]
    lane = lax.broadcasted_iota(jnp.int32, (tq, LANES), 1)
    row = lax.broadcasted_iota(jnp.int32, (tk, tk), 0)
    col = lax.broadcasted_iota(jnp.int32, (tk, tk), 1)
    diag_mask = col < row
    n_meta_pad = km_ref.shape[0]
    meta_mask = lax.broadcasted_iota(jnp.int32, (tq, n_meta_pad), 1) < N_META
    tri = tri_ref[...]
    tri_meta = tri_ref[0:n_meta_pad, 0:n_meta_pad]
    heads = (0, 1)
    bias = [bias_ref[2 * j + p] * LOG2_E for p in heads]
    qm = [jnp.where((lane >= dh) if p else (lane < dh), q, jnp.zeros_like(q)) for p in heads]

    def key_block(block):
        start = pl.multiple_of(block * tk, tk)
        return k_ref[0, pl.ds(start, tk), :], v_ref[0, pl.ds(start, tk), :]

    for r in range(n_sub):
        rows = slice(r * tk, (r + 1) * tk)
        for p in heads:
            acc, later = 0.0, 0.0
            for c in range(r, -1, -1):
                kb, vb = key_block(i * n_sub + c)
                z = _dot_nt(qm[p][rows], kb) + bias[p]
                pv, later = _stick_block(z, vb, diag_mask if c == r else None, tri, later)
                acc = acc + pv
            acc_ref[p, rows, :] = acc
            later_ref[p, rows, :] = later

    def earlier_block(n, carry):
        kb, vb = key_block(i * n_sub - 1 - n)
        for p in heads:
            z = _dot_nt(qm[p], kb) + bias[p]
            pv, later = _stick_block(z, vb, None, tri, later_ref[p])
            acc_ref[p] += pv
            later_ref[p] = later
        return carry

    lax.fori_loop(0, i * n_sub, earlier_block, 0)

    res = []
    for p in heads:
        z = _dot_nt(qm[p], km_ref[...]) + bias[p]
        pv, _ = _stick_block(z, vm_ref[...], meta_mask, tri_meta, later_ref[p])
        res.append(acc_ref[p] + pv)
    o_ref[0] = jnp.where(lane < dh, res[0], res[1]).astype(o_ref.dtype)


def _attn_prompt(sb_bias, q, kb, vb, km, vm, tq, tk):
    b, s, d_attn = q.shape
    tri = _strict_lower_ones(tk)
    n_meta_pad = km.shape[0]
    return pl.pallas_call(
        _attn_kernel,
        grid=(b, d_attn // LANES, s // tq),
        in_specs=[pl.BlockSpec(memory_space=pltpu.SMEM),
                  pl.BlockSpec((1, tq, LANES), lambda b, j, i: (b, i, j)),
                  pl.BlockSpec((1, s, LANES), lambda b, j, i: (b, 0, j)),
                  pl.BlockSpec((1, s, LANES), lambda b, j, i: (b, 0, j)),
                  pl.BlockSpec((n_meta_pad, LANES), lambda b, j, i: (0, j)),
                  pl.BlockSpec((n_meta_pad, LANES), lambda b, j, i: (0, j)),
                  pl.BlockSpec((tk, tk), lambda b, j, i: (0, 0))],
        out_specs=pl.BlockSpec((1, tq, LANES), lambda b, j, i: (b, i, j)),
        out_shape=jax.ShapeDtypeStruct((b, s, d_attn), BF16),
        scratch_shapes=[pltpu.VMEM((2, tq, LANES), F32), pltpu.VMEM((2, tq, 1), F32)],
        compiler_params=pltpu.CompilerParams(
            dimension_semantics=("arbitrary", "arbitrary", "arbitrary"),
            vmem_limit_bytes=VMEM_LIMIT),
        name="attn_prompt",
    )(sb_bias, q, kb, vb, km, vm, tri)


def _attn_sample_kernel(pt_ref, q_ref, knew_ref, vnew_ref, bias_ref, tri_ref, *refs, past_len):
    del pt_ref
    npg = (len(refs) - 4) // 2
    k_refs, v_refs = refs[:npg], refs[npg:2 * npg]
    o_ref, acc_ref, new_ref, later_ref = refs[2 * npg:]
    g = pl.program_id(1)
    d_attn = q_ref.shape[-1]
    dh = d_attn // H_A
    n_new = knew_ref.shape[1]
    head = lax.broadcasted_iota(jnp.int32, (H_A, d_attn), 0)
    lane = lax.broadcasted_iota(jnp.int32, (H_A, d_attn), 1)
    own = (lane >= head * dh) & (lane < (head + 1) * dh)
    qmat_f32 = jnp.where(own, q_ref[0].astype(F32), 0.0)
    qmat = qmat_f32.astype(BF16)
    bias = bias_ref[...] * LOG2_E
    tri = tri_ref[...]

    @pl.when(g == 0)
    def _():
        as_dot_operand = lambda a: a.astype(BF16).astype(F32)
        k_pos = past_len + lax.broadcasted_iota(jnp.int32, (H_A, n_new), 1)
        q_pos = past_len + n_new - 1
        visible = k_pos < q_pos
        z = jnp.sum(qmat_f32 * as_dot_operand(knew_ref[0]), axis=1, keepdims=True) + bias
        softplus, log_beta = _softplus_terms(z)
        a = jnp.where(visible, jnp.exp2(log_beta), 0.0)
        av = jnp.where(own, as_dot_operand(a) * as_dot_operand(vnew_ref[0]), 0.0)
        new_ref[...] = jnp.sum(av, axis=0, keepdims=True)
        later_ref[...] = jnp.where(visible, softplus, 0.0)
        acc_ref[...] = jnp.zeros_like(acc_ref)

    order = tuple(reversed(range(npg)))
    z = jnp.concatenate([_dot(qmat, k_refs[r][0].astype(BF16)) + bias for r in order], axis=0)
    softplus, log_beta = _softplus_terms(z)
    hi = softplus.astype(BF16)
    lo = (softplus - hi.astype(F32)).astype(BF16)
    suffix = _dot(hi, tri) + _dot(lo, tri)
    page_sum = jnp.sum(softplus, axis=1, keepdims=True)
    later = later_ref[...]
    weights = []
    for m in range(npg):
        rows = slice(m * H_A, (m + 1) * H_A)
        weights.append(jnp.exp2(log_beta[rows] - suffix[rows] - later))
        later = later + page_sum[rows]
    later_ref[...] = later

    for h in range(H_A):
        rows = slice(h * dh, (h + 1) * dh)
        acc = acc_ref[rows, :]
        for m, r in enumerate(order):
            acc = acc + v_refs[r][0, rows, :] * weights[m][h:h + 1, :]
        acc_ref[rows, :] = acc

    @pl.when(g == pl.num_programs(1) - 1)
    def _():
        o = jnp.sum(acc_ref[...].T, axis=0, keepdims=True) + new_ref[...]
        o_ref[0] = o.astype(o_ref.dtype)


def _attn_sample(sb_bias, q, k_new, v_new, cache_k, cache_v, page_table):
    db, n_new, d_attn = q.shape
    n_pool, page = cache_k.shape[0], cache_k.shape[1]
    n_pages = page_table.shape[1]
    npg = PAGES_PER_STEP
    n_steps = n_pages // npg
    ck = cache_k.transpose(0, 2, 3, 1).reshape(n_pool, d_attn, page)
    cv = cache_v.transpose(0, 2, 3, 1).reshape(n_pool, d_attn, page)
    tri = _strict_lower_ones(page)
    bias = sb_bias.reshape(H_A, 1)

    def page_spec(r):
        return pl.BlockSpec((1, d_attn, page),
                            lambda b, g, pt: (pt[b, (n_steps - 1 - g) * npg + r], 0, 0))

    tok_spec = pl.BlockSpec((1, n_new, d_attn), lambda b, g, pt: (b, 0, 0))
    grid_spec = pltpu.PrefetchScalarGridSpec(
        num_scalar_prefetch=1,
        grid=(db, n_steps),
        in_specs=[tok_spec, tok_spec, tok_spec,
                  pl.BlockSpec((H_A, 1), lambda b, g, pt: (0, 0)),
                  pl.BlockSpec((page, page), lambda b, g, pt: (0, 0))]
                 + [page_spec(r) for r in range(npg)] + [page_spec(r) for r in range(npg)],
        out_specs=tok_spec,
        scratch_shapes=[pltpu.VMEM((d_attn, page), F32), pltpu.VMEM((n_new, d_attn), F32),
                        pltpu.VMEM((H_A, 1), F32)],
    )
    return pl.pallas_call(
        functools.partial(_attn_sample_kernel, past_len=n_pages * page),
        grid_spec=grid_spec,
        out_shape=jax.ShapeDtypeStruct((db, n_new, d_attn), BF16),
        compiler_params=pltpu.CompilerParams(dimension_semantics=("arbitrary", "arbitrary"),
                                             vmem_limit_bytes=VMEM_LIMIT),
        name="attn_sample",
    )(page_table, q, k_new, v_new, bias, tri, *([ck] * npg), *([cv] * npg))


def _pool_heads(window_sums, cur, inv_cnt, pw_ref, ps_ref):
    outs = []
    for g in range(len(POOL_WINDOWS)):
        d = window_sums[g] * inv_cnt[g] - cur[g]
        outs.append(_dot(d.astype(BF16), pw_ref[g]) * ps_ref[g:g + 1, :])
    return jnp.concatenate(outs, axis=1)


def _pool_tile(uext_ref, tm, pos0, pw_ref, ps_ref):
    ch = pw_ref.shape[-1]
    pos = pos0 + lax.broadcasted_iota(jnp.int32, (tm, ch), 0)
    sums, cur, inv = [], [], []
    for g, w in enumerate(POOL_WINDOWS):
        lanes = slice(g * ch, (g + 1) * ch)
        c = uext_ref[pl.ds(POOL_HALO, tm), lanes]
        s = c
        for back in range(1, w):
            s = s + uext_ref[pl.ds(POOL_HALO - back, tm), lanes]
        cur.append(c)
        sums.append(s)
        inv.append(1.0 / jnp.minimum(w, pos + 1).astype(F32))
    return _pool_heads(sums, cur, inv, pw_ref, ps_ref)


def _mix_residual(x, o_attn, o_pool, wout_ref):
    d_attn = o_attn.shape[-1]
    mix = _dot(o_attn, wout_ref[0:d_attn, :]) + _dot(o_pool.astype(BF16), wout_ref[d_attn:, :])
    return x + mix


def _conv3(cw_ref, cb_ref, cols, two_back, one_back, cur):
    return (cb_ref[:, cols] + cw_ref[0:1, cols] * two_back + cw_ref[1:2, cols] * one_back
            + cw_ref[2:3, cols] * cur)


def _mixffn_kernel(x_ref, oa_ref, u_ref, umeta_ref, upmeta_ref, pw_ref, ps_ref, wout_ref, g2_ref,
                   wup_ref, cw_ref, cb_ref, wdown_ref, g3_ref, y_ref, conv_ref,
                   uext_ref, carry_ref, ext_ref):
    i = pl.program_id(1)
    tm = x_ref.shape[1]
    d_ff = wdown_ref.shape[0]

    @pl.when(i == 0)
    def _():
        uext_ref[0:POOL_HALO, :] = umeta_ref[...]
        carry_ref[...] = upmeta_ref[...]

    @pl.when(i > 0)
    def _():
        uext_ref[0:POOL_HALO, :] = uext_ref[tm:tm + POOL_HALO, :]

    uext_ref[POOL_HALO:POOL_HALO + tm, :] = u_ref[0]
    o_pool = _pool_tile(uext_ref, tm, N_META + i * tm, pw_ref, ps_ref)
    xp = _mix_residual(x_ref[0], oa_ref[0], o_pool, wout_ref)
    h = _rmsnorm(xp, g2_ref[...]).astype(BF16)

    f = jnp.zeros_like(xp)
    for c in range(d_ff // FFN_CHUNK):
        halves = []
        for part in range(2):
            cols = slice(part * d_ff + c * FFN_CHUNK, part * d_ff + (c + 1) * FFN_CHUNK)
            up = _dot(h, wup_ref[:, cols])
            ext_ref[part, 0:CONV_HALO, :] = carry_ref[:, cols]
            ext_ref[part, CONV_HALO:CONV_HALO + tm, :] = up
            carry_ref[:, cols] = up[tm - CONV_HALO:, :]
            conv_ref[0, :, cols] = ext_ref[part, pl.ds(CONV_HALO + tm - (CONV_W - 1), CONV_W - 1), :]
            halves.append(_conv3(cw_ref, cb_ref, cols,
                                 ext_ref[part, pl.ds(CONV_HALO - 2, tm), :],
                                 ext_ref[part, pl.ds(CONV_HALO - 1, tm), :], up))
        gate, val = halves
        act = gate * jax.nn.sigmoid(gate) * val
        f = f + _dot(act.astype(BF16), wdown_ref[c * FFN_CHUNK:(c + 1) * FFN_CHUNK, :])
    y_ref[0] = _rmsnorm(xp + f, g3_ref[...])


def _mixffn_prompt(x, o_attn, u, u_meta, up_meta_tail, weights, tm):
    b, s, d = x.shape
    pool_w, pool_scale, w_out, g2, w_up, conv_w, conv_b, w_down, g3 = weights
    d_attn, d_pool, d_ff2 = o_attn.shape[-1], u.shape[-1], w_up.shape[1]
    const = lambda a: pl.BlockSpec(a.shape, lambda b, i: (0,) * a.ndim, pipeline_mode=pl.Buffered(1))
    row_spec = lambda width: pl.BlockSpec((1, tm, width), lambda b, i: (b, i, 0))
    return pl.pallas_call(
        _mixffn_kernel,
        grid=(b, s // tm),
        in_specs=[row_spec(d), row_spec(d_attn), row_spec(d_pool), const(u_meta), const(up_meta_tail),
                  const(pool_w), const(pool_scale), const(w_out), const(g2), const(w_up),
                  const(conv_w), const(conv_b), const(w_down), const(g3)],
        out_specs=[row_spec(d), pl.BlockSpec((1, CONV_W - 1, d_ff2), lambda b, i: (b, 0, 0))],
        out_shape=[jax.ShapeDtypeStruct((b, s, d), F32),
                   jax.ShapeDtypeStruct((b, CONV_W - 1, d_ff2), F32)],
        scratch_shapes=[pltpu.VMEM((POOL_HALO + tm, d_pool), F32),
                        pltpu.VMEM((CONV_HALO, d_ff2), F32),
                        pltpu.VMEM((2, CONV_HALO + tm, FFN_CHUNK), F32)],
        compiler_params=pltpu.CompilerParams(dimension_semantics=("arbitrary", "arbitrary"),
                                             vmem_limit_bytes=VMEM_LIMIT),
        name="mixffn_prompt",
    )(x, o_attn, u, u_meta, up_meta_tail, pool_w, pool_scale, w_out, g2, w_up, conv_w, conv_b, w_down, g3)


def _meta_kernel(bias_ref, x_ref, q_ref, kb_ref, vb_ref, u_ref, pw_ref, ps_ref, wout_ref, g2_ref,
                 wup_ref, up_ref, uext_ref):
    n, d_attn = q_ref.shape
    n_keys = kb_ref.shape[0]
    dh = LANES // 2
    lane = lax.broadcasted_iota(jnp.int32, (n, LANES), 1)
    row = lax.broadcasted_iota(jnp.int32, (n, n_keys), 0)
    col = lax.broadcasted_iota(jnp.int32, (n, n_keys), 1)
    causal = col < row
    tri = _strict_lower_ones(n_keys)
    pairs = []
    for j in range(d_attn // LANES):
        cols = slice(j * LANES, (j + 1) * LANES)
        q, kb, vb = q_ref[:, cols], kb_ref[:, cols], vb_ref[:, cols]
        out = None
        for p in range(2):
            in_head = (lane >= dh) if p else (lane < dh)
            qm = jnp.where(in_head, q, jnp.zeros_like(q))
            z = _dot_nt(qm, kb) + bias_ref[2 * j + p] * LOG2_E
            pv, _ = _stick_block(z, vb, causal, tri, 0.0)
            out = pv if p == 0 else jnp.where(lane < dh, out, pv)
        pairs.append(out)
    o_attn = jnp.concatenate(pairs, axis=1).astype(BF16)

    uext_ref[0:POOL_HALO, :] = jnp.zeros((POOL_HALO, uext_ref.shape[1]), F32)
    uext_ref[POOL_HALO:POOL_HALO + n, :] = u_ref[...]
    o_pool = _pool_tile(uext_ref, n, 0, pw_ref, ps_ref)
    xp = _mix_residual(x_ref[...], o_attn, o_pool, wout_ref)
    h = _rmsnorm(xp, g2_ref[...]).astype(BF16)
    up_ref[...] = _dot(h, wup_ref[...])


def _meta_rows(sb_bias, x, q, kb, vb, u, weights):
    pool_w, pool_scale, w_out, g2, w_up = weights
    n = x.shape[0]
    vmem = pl.BlockSpec(memory_space=pltpu.VMEM)
    return pl.pallas_call(
        _meta_kernel,
        in_specs=[pl.BlockSpec(memory_space=pltpu.SMEM)] + [vmem] * 10,
        out_specs=vmem,
        out_shape=jax.ShapeDtypeStruct((n, w_up.shape[1]), F32),
        scratch_shapes=[pltpu.VMEM((POOL_HALO + n, u.shape[1]), F32)],
        compiler_params=pltpu.CompilerParams(vmem_limit_bytes=VMEM_LIMIT),
        name="meta_rows",
    )(sb_bias, x, q, kb, vb, u, pool_w, pool_scale, w_out, g2, w_up)


def _mixffn_sample_kernel(x_ref, oa_ref, u_ref, pstate_ref, cstate_ref, pw_ref, ps_ref, wout_ref,
                          g2_ref, wup_ref, cw_ref, cb_ref, wdown_ref, g3_ref, y_ref, up_ref, *, pos):
    n_state = pstate_ref.shape[0]
    ch = pw_ref.shape[-1]
    d_ff = wdown_ref.shape[0]
    sums, cur, inv = [], [], []
    for g, w in enumerate(POOL_WINDOWS):
        lanes = slice(g * ch, (g + 1) * ch)
        c = u_ref[:, lanes]
        s = c
        for back in range(1, w):
            s = s + pstate_ref[n_state - back, :, lanes]
        cur.append(c)
        sums.append(s)
        inv.append(1.0 / min(w, pos + 1))
    o_pool = _pool_heads(sums, cur, inv, pw_ref, ps_ref)
    xp = _mix_residual(x_ref[...], oa_ref[...], o_pool, wout_ref)
    h = _rmsnorm(xp, g2_ref[...]).astype(BF16)
    up = _dot(h, wup_ref[...])
    up_ref[...] = up
    c = _conv3(cw_ref, cb_ref, slice(None), cstate_ref[0], cstate_ref[1], up)
    gate, val = c[:, :d_ff], c[:, d_ff:]
    act = gate * jax.nn.sigmoid(gate) * val
    f = _dot(act.astype(BF16), wdown_ref[...])
    y_ref[...] = _rmsnorm(xp + f, g3_ref[...])


def _mixffn_sample(x, o_attn, u, pool_state_t, conv_state_t, weights, pos):
    pool_w, pool_scale, w_out, g2, w_up, conv_w, conv_b, w_down, g3 = weights
    n, d = x.shape
    vmem = pl.BlockSpec(memory_space=pltpu.VMEM)
    return pl.pallas_call(
        functools.partial(_mixffn_sample_kernel, pos=pos),
        in_specs=[vmem] * 14,
        out_specs=[vmem, vmem],
        out_shape=[jax.ShapeDtypeStruct((n, d), F32), jax.ShapeDtypeStruct((n, w_up.shape[1]), F32)],
        compiler_params=pltpu.CompilerParams(vmem_limit_bytes=VMEM_LIMIT),
        name="mixffn_sample",
    )(x, o_attn, u, pool_state_t, conv_state_t, pool_w, pool_scale, w_out, g2, w_up, conv_w, conv_b,
      w_down, g3)


def kernel(x_prompt, x_sample, cache_k, cache_v, state_pool, state_conv, page_table, meta_tokens,
           norm_mix_g, w_in, sb_bias, pool_w, pool_scale, w_out, norm_ffn_g, w_up, conv_w, conv_b,
           w_down, norm_final_g):
    b, s, d = x_prompt.shape
    db, n_new, _ = x_sample.shape
    assert n_new == 1, "the sample path handles one new token per sequence"
    dh = cache_k.shape[-1]
    d_attn = H_A * dh
    d_pool = state_pool.shape[-1]
    past_len = page_table.shape[1] * cache_k.shape[1]
    t = N_META + s

    g1 = norm_mix_g.reshape(1, d)
    g2 = norm_ffn_g.reshape(1, d)
    g3 = norm_final_g.reshape(1, d)
    w_in_b, w_out_b, w_up_b, w_down_b = (w.astype(BF16) for w in (w_in, w_out, w_up, w_down))
    pool_w_b = pool_w.astype(BF16)
    conv_b2 = conv_b.reshape(1, -1)
    sb_bias = sb_bias.astype(F32)
    mix_weights = (pool_w_b, pool_scale, w_out_b, g2, w_up_b)
    ffn_weights = mix_weights + (conv_w, conv_b2, w_down_b, g3)

    q_m, k_m, v_m, u_m, kb_m, vb_m = _proj(meta_tokens, g1, w_in_b, d_attn, N_META)
    pad_meta = lambda a: jnp.pad(a, ((0, LANES - N_META), (0, 0)))
    kb_m, vb_m = pad_meta(kb_m), pad_meta(vb_m)
    up_m = _meta_rows(sb_bias, meta_tokens, q_m, kb_m, vb_m, u_m, mix_weights)

    q_r, k_r, v_r, u_r, kb_r, vb_r = _proj(x_prompt.reshape(b * s, d), g1, w_in_b, d_attn, PROJ_ROWS)
    seq = lambda a: a.reshape(b, s, a.shape[-1])
    o_attn = _attn_prompt(sb_bias, seq(q_r), seq(kb_r), seq(vb_r), kb_m, vb_m,
                          ATTN_Q_TILE, ATTN_K_BLOCK)
    y_prompt, conv_prompt = _mixffn_prompt(x_prompt, o_attn, seq(u_r), u_m, up_m[N_META - CONV_HALO:],
                                           ffn_weights, FFN_ROWS)
    with_meta = lambda m, r: jnp.concatenate(
        [jnp.broadcast_to(m[None], (b,) + m.shape), seq(r)], axis=1).reshape(b, t, H_A, dh)
    k_prompt = with_meta(k_m, k_r)
    v_prompt = with_meta(v_m, v_r)
    pool_prompt = seq(u_r)[:, s - state_pool.shape[1]:]

    q_s, k_s, v_s, u_s, _, _ = _proj(x_sample.reshape(db * n_new, d), g1, w_in_b, d_attn, db * n_new)
    tok = lambda a: a.reshape(db, n_new, a.shape[-1])
    o_attn_s = _attn_sample(sb_bias, tok(q_s), tok(k_s), tok(v_s), cache_k, cache_v, page_table)
    y_s, up_s = _mixffn_sample(x_sample.reshape(db, d), o_attn_s.reshape(db, d_attn), u_s,
                               jnp.swapaxes(state_pool, 0, 1), jnp.swapaxes(state_conv, 0, 1),
                               ffn_weights, past_len)
    y_sample = y_s.reshape(db, n_new, d)
    k_sample = k_s.reshape(db, n_new, H_A, dh)
    v_sample = v_s.reshape(db, n_new, H_A, dh)
    pool_sample = jnp.concatenate([state_pool[:, 1:], u_s.reshape(db, n_new, d_pool)], axis=1)
    conv_sample = jnp.concatenate([state_conv[:, 1:], up_s[:, None, :]], axis=1)

    return (y_prompt, y_sample, k_prompt, v_prompt, pool_prompt, conv_prompt,
            k_sample, v_sample, pool_sample, conv_sample)
```

```python
import functools

import jax
import jax.numpy as jnp
from jax import lax
from jax.experimental import pallas as pl
from jax.experimental.pallas import tpu as pltpu

F32 = jnp.float32
BF16 = jnp.bfloat16

N_META = 16
H_A = 8
POOL_WINDOWS = (2, 4, 8, 16)
CONV_W = 3
EPS = 1e-6
LOG2_E = 1.4426950408889634

LANES = 128
SUBLANES = 8
VMEM_LIMIT = 56 * 1024 * 1024

POOL_HALO = 16
CONV_HALO = SUBLANES

PROJ_ROWS = 512
ATTN_Q_TILE = 512
ATTN_K_BLOCK = 256
FFN_ROWS = 256
FFN_CHUNK = 256
PAGES_PER_STEP = 8


def _rmsnorm(x, g):
    r = lax.rsqrt(jnp.mean(x * x, axis=-1, keepdims=True) + EPS)
    return x * r * g


def _dot(a, b):
    return jnp.dot(a, b, preferred_element_type=F32)


def _dot_nt(a, b):
    return lax.dot_general(a, b, (((1,), (1,)), ((), ())), preferred_element_type=F32)


def _strict_lower_ones(n):
    j = lax.broadcasted_iota(jnp.int32, (n, n), 0)
    s = lax.broadcasted_iota(jnp.int32, (n, n), 1)
    return (j > s).astype(BF16)


def _softplus_terms(z):
    softplus = jnp.maximum(z, 0.0) + jnp.log2(1.0 + jnp.exp2(-jnp.abs(z)))
    return softplus, z - softplus


def _stick_block(z, vb, mask, tri, later):
    softplus, log_beta = _softplus_terms(z)
    if mask is not None:
        softplus = jnp.where(mask, softplus, 0.0)
    suffix = _dot(softplus.astype(BF16), tri)
    a = jnp.exp2(log_beta - suffix)
    if mask is not None:
        a = jnp.where(mask, a, 0.0)
    pv = _dot(a.astype(BF16), vb)
    block_sum = jnp.sum(softplus, axis=1, keepdims=True)
    if later is None:
        return pv, block_sum
    return pv * jnp.exp2(-later), later + block_sum


def _proj_kernel(x_ref, g_ref, w_ref, q_ref, k_ref, v_ref, u_ref, kb_ref, vb_ref, *, scale):
    d_attn = q_ref.shape[-1]
    h = _rmsnorm(x_ref[...], g_ref[...]).astype(BF16)
    p = _dot(h, w_ref[...])
    k = p[:, d_attn:2 * d_attn]
    v = p[:, 2 * d_attn:3 * d_attn]
    q_ref[...] = (p[:, :d_attn] * scale).astype(BF16)
    k_ref[...] = k
    v_ref[...] = v
    u_ref[...] = p[:, 3 * d_attn:]
    kb_ref[...] = k.astype(BF16)
    vb_ref[...] = v.astype(BF16)


def _proj(x, g, w_in, d_attn, tm):
    rows, d = x.shape
    n = w_in.shape[1]
    d_pool = n - 3 * d_attn
    dh = d_attn // H_A
    row_spec = lambda width: pl.BlockSpec((tm, width), lambda i: (i, 0))
    return pl.pallas_call(
        functools.partial(_proj_kernel, scale=dh ** -0.5 * LOG2_E),
        grid=(rows // tm,),
        in_specs=[row_spec(d),
                  pl.BlockSpec((1, d), lambda i: (0, 0)),
                  pl.BlockSpec((d, n), lambda i: (0, 0))],
        out_specs=[row_spec(d_attn), row_spec(d_attn), row_spec(d_attn), row_spec(d_pool),
                   row_spec(d_attn), row_spec(d_attn)],
        out_shape=[jax.ShapeDtypeStruct((rows, d_attn), BF16),
                   jax.ShapeDtypeStruct((rows, d_attn), F32),
                   jax.ShapeDtypeStruct((rows, d_attn), F32),
                   jax.ShapeDtypeStruct((rows, d_pool), F32),
                   jax.ShapeDtypeStruct((rows, d_attn), BF16),
                   jax.ShapeDtypeStruct((rows, d_attn), BF16)],
        compiler_params=pltpu.CompilerParams(dimension_semantics=("arbitrary",),
                                             vmem_limit_bytes=VMEM_LIMIT),
        name="proj",
    )(x, g, w_in)


def _attn_kernel(bias_ref, q_ref, k_ref, v_ref, km_ref, vm_ref, tri_ref, o_ref, acc_ref, later_ref):
    j = pl.program_id(1)
    i = pl.program_id(2)
    tq = q_ref.shape[1]
    tk = tri_ref.shape[0]
    n_sub = tq // tk
    dh = LANES // 2
    q = q_ref[0]
    lane = lax.broadcasted_iota(jnp.int32, (tq, LANES), 1)
    row = lax.broadcasted_iota(jnp.int32, (tk, tk), 0)
    col = lax.broadcasted_iota(jnp.int32, (tk, tk), 1)
    diag_mask = col < row
    n_meta_pad = km_ref.shape[0]
    meta_mask = lax.broadcasted_iota(jnp.int32, (tq, n_meta_pad), 1) < N_META
    tri = tri_ref[...]
    tri_meta = tri_ref[0:n_meta_pad, 0:n_meta_pad]
    heads = (0, 1)
    bias = [bias_ref[2 * j + p] * LOG2_E for p in heads]
    qm = [jnp.where((lane >= dh) if p else (lane < dh), q, jnp.zeros_like(q)) for p in heads]

    def key_block(block):
        start = pl.multiple_of(block * tk, tk)
        return k_ref[0, pl.ds(start, tk), :], v_ref[0, pl.ds(start, tk), :]

    for r in range(n_sub):
        rows = slice(r * tk, (r + 1) * tk)
        for p in heads:
            acc, later = 0.0, None
            for c in range(r, -1, -1):
                kb, vb = key_block(i * n_sub + c)
                z = _dot_nt(qm[p][rows], kb) + bias[p]
                pv, later = _stick_block(z, vb, diag_mask if c == r else None, tri, later)
                acc = acc + pv
            acc_ref[p, rows, :] = acc
            later_ref[p, rows, :] = later

    def earlier_block(n, carry):
        kb, vb = key_block(i * n_sub - 1 - n)
        for p in heads:
            z = _dot_nt(qm[p], kb) + bias[p]
            pv, later = _stick_block(z, vb, None, tri, later_ref[p])
            acc_ref[p] += pv
            later_ref[p] = later
        return carry

    lax.fori_loop(0, i * n_sub, earlier_block, 0)

    res = []
    for p in heads:
        z = _dot_nt(qm[p], km_ref[...]) + bias[p]
        pv, _ = _stick_block(z, vm_ref[...], meta_mask, tri_meta, later_ref[p])
        res.append(acc_ref[p] + pv)
    o_ref[0] = jnp.where(lane < dh, res[0], res[1]).astype(o_ref.dtype)


def _attn_prompt(sb_bias, q, kb, vb, km, vm, tq, tk):
    b, s, d_attn = q.shape
    tri = _strict_lower_ones(tk)
    n_meta_pad = km.shape[0]
    return pl.pallas_call(
        _attn_kernel,
        grid=(b, d_attn // LANES, s // tq),
        in_specs=[pl.BlockSpec(memory_space=pltpu.SMEM),
                  pl.BlockSpec((1, tq, LANES), lambda b, j, i: (b, i, j)),
                  pl.BlockSpec((1, s, LANES), lambda b, j, i: (b, 0, j)),
                  pl.BlockSpec((1, s, LANES), lambda b, j, i: (b, 0, j)),
                  pl.BlockSpec((n_meta_pad, LANES), lambda b, j, i: (0, j)),
                  pl.BlockSpec((n_meta_pad, LANES), lambda b, j, i: (0, j)),
                  pl.BlockSpec((tk, tk), lambda b, j, i: (0, 0))],
        out_specs=pl.BlockSpec((1, tq, LANES), lambda b, j, i: (b, i, j)),
        out_shape=jax.ShapeDtypeStruct((b, s, d_attn), BF16),
        scratch_shapes=[pltpu.VMEM((2, tq, LANES), F32), pltpu.VMEM((2, tq, 1), F32)],
        compiler_params=pltpu.CompilerParams(
            dimension_semantics=("arbitrary", "arbitrary", "arbitrary"),
            vmem_limit_bytes=VMEM_LIMIT),
        name="attn_prompt",
    )(sb_bias, q, kb, vb, km, vm, tri)


def _attn_sample_kernel(pt_ref, q_ref, knew_ref, vnew_ref, bias_ref, tri_ref, *refs, past_len):
    del pt_ref
    npg = (len(refs) - 4) // 2
    k_refs, v_refs = refs[:npg], refs[npg:2 * npg]
    o_ref, acc_ref, new_ref, later_ref = refs[2 * npg:]
    g = pl.program_id(1)
    d_attn = q_ref.shape[-1]
    dh = d_attn // H_A
    n_new = knew_ref.shape[1]
    head = lax.broadcasted_iota(jnp.int32, (H_A, d_attn), 0)
    lane = lax.broadcasted_iota(jnp.int32, (H_A, d_attn), 1)
    own = (lane >= head * dh) & (lane < (head + 1) * dh)
    qmat_f32 = jnp.where(own, q_ref[0].astype(F32), 0.0)
    qmat = qmat_f32.astype(BF16)
    bias = bias_ref[...] * LOG2_E
    tri = tri_ref[...]

    @pl.when(g == 0)
    def _():
        as_dot_operand = lambda a: a.astype(BF16).astype(F32)
        k_pos = past_len + lax.broadcasted_iota(jnp.int32, (H_A, n_new), 1)
        q_pos = past_len + n_new - 1
        visible = k_pos < q_pos
        z = jnp.sum(qmat_f32 * as_dot_operand(knew_ref[0]), axis=1, keepdims=True) + bias
        softplus, log_beta = _softplus_terms(z)
        a = jnp.where(visible, jnp.exp2(log_beta), 0.0)
        av = jnp.where(own, as_dot_operand(a) * as_dot_operand(vnew_ref[0]), 0.0)
        new_ref[...] = jnp.sum(av, axis=0, keepdims=True)
        later_ref[...] = jnp.where(visible, softplus, 0.0)
        acc_ref[...] = jnp.zeros_like(acc_ref)

    order = tuple(reversed(range(npg)))
    z = jnp.concatenate([_dot(qmat, k_refs[r][0].astype(BF16)) + bias for r in order], axis=0)
    softplus, log_beta = _softplus_terms(z)
    hi = softplus.astype(BF16)
    lo = (softplus - hi.astype(F32)).astype(BF16)
    suffix = _dot(hi, tri) + _dot(lo, tri)
    page_sum = jnp.sum(softplus, axis=1, keepdims=True)
    later = later_ref[...]
    weights = []
    for m in range(npg):
        rows = slice(m * H_A, (m + 1) * H_A)
        weights.append(jnp.exp2(log_beta[rows] - suffix[rows] - later))
        later = later + page_sum[rows]
    later_ref[...] = later

    for h in range(H_A):
        rows = slice(h * dh, (h + 1) * dh)
        acc = acc_ref[rows, :]
        for m, r in enumerate(order):
            acc = acc + v_refs[r][0, rows, :] * weights[m][h:h + 1, :]
        acc_ref[rows, :] = acc

    @pl.when(g == pl.num_programs(1) - 1)
    def _():
        o = jnp.sum(acc_ref[...].T, axis=0, keepdims=True) + new_ref[...]
        o_ref[0] = o.astype(o_ref.dtype)


def _attn_sample(sb_bias, q, k_new, v_new, cache_k, cache_v, page_table):
    db, n_new, d_attn = q.shape
    n_pool, page = cache_k.shape[0], cache_k.shape[1]
    n_pages = page_table.shape[1]
    npg = PAGES_PER_STEP
    n_steps = n_pages // npg
    ck = cache_k.transpose(0, 2, 3, 1).reshape(n_pool, d_attn, page)
    cv = cache_v.transpose(0, 2, 3, 1).reshape(n_pool, d_attn, page)
    tri = _strict_lower_ones(page)
    bias = sb_bias.reshape(H_A, 1)

    def page_spec(r):
        return pl.BlockSpec((1, d_attn, page),
                            lambda b, g, pt: (pt[b, (n_steps - 1 - g) * npg + r], 0, 0))

    tok_spec = pl.BlockSpec((1, n_new, d_attn), lambda b, g, pt: (b, 0, 0))
    grid_spec = pltpu.PrefetchScalarGridSpec(
        num_scalar_prefetch=1,
        grid=(db, n_steps),
        in_specs=[tok_spec, tok_spec, tok_spec,
                  pl.BlockSpec((H_A, 1), lambda b, g, pt: (0, 0)),
                  pl.BlockSpec((page, page), lambda b, g, pt: (0, 0))]
                 + [page_spec(r) for r in range(npg)] + [page_spec(r) for r in range(npg)],
        out_specs=tok_spec,
        scratch_shapes=[pltpu.VMEM((d_attn, page), F32), pltpu.VMEM((n_new, d_attn), F32),
                        pltpu.VMEM((H_A, 1), F32)],
    )
    return pl.pallas_call(
        functools.partial(_attn_sample_kernel, past_len=n_pages * page),
        grid_spec=grid_spec,
        out_shape=jax.ShapeDtypeStruct((db, n_new, d_attn), BF16),
        compiler_params=pltpu.CompilerParams(dimension_semantics=("arbitrary", "arbitrary"),
                                             vmem_limit_bytes=VMEM_LIMIT),
        name="attn_sample",
    )(page_table, q, k_new, v_new, bias, tri, *([ck] * npg), *([cv] * npg))


def _pool_heads(window_sums, cur, inv_cnt, pw_ref, ps_ref):
    outs = []
    for g in range(len(POOL_WINDOWS)):
        d = window_sums[g] * inv_cnt[g] - cur[g]
        outs.append(_dot(d.astype(BF16), pw_ref[g]) * ps_ref[g:g + 1, :])
    return jnp.concatenate(outs, axis=1)


def _pool_tile(uext_ref, tm, pos0, pw_ref, ps_ref):
    ch = pw_ref.shape[-1]
    pos = pos0 + lax.broadcasted_iota(jnp.int32, (tm, ch), 0)
    sums, cur, inv = [], [], []
    for g, w in enumerate(POOL_WINDOWS):
        lanes = slice(g * ch, (g + 1) * ch)
        c = uext_ref[pl.ds(POOL_HALO, tm), lanes]
        s = c
        for back in range(1, w):
            s = s + uext_ref[pl.ds(POOL_HALO - back, tm), lanes]
        cur.append(c)
        sums.append(s)
        inv.append(1.0 / jnp.minimum(w, pos + 1).astype(F32))
    return _pool_heads(sums, cur, inv, pw_ref, ps_ref)


def _mix_residual(x, o_attn, o_pool, wout_ref):
    d_attn = o_attn.shape[-1]
    mix = _dot(o_attn, wout_ref[0:d_attn, :]) + _dot(o_pool.astype(BF16), wout_ref[d_attn:, :])
    return x + mix


def _conv3(cw_ref, cb_ref, cols, two_back, one_back, cur):
    return (cb_ref[:, cols] + cw_ref[0:1, cols] * two_back + cw_ref[1:2, cols] * one_back
            + cw_ref[2:3, cols] * cur)


def _mixffn_kernel(x_ref, oa_ref, u_ref, umeta_ref, upmeta_ref, pw_ref, ps_ref, wout_ref, g2_ref,
                   wup_ref, cw_ref, cb_ref, wdown_ref, g3_ref, y_ref, conv_ref,
                   uext_ref, carry_ref, ext_ref):
    i = pl.program_id(1)
    tm = x_ref.shape[1]
    d_ff = wdown_ref.shape[0]

    @pl.when(i == 0)
    def _():
        uext_ref[0:POOL_HALO, :] = umeta_ref[...]
        carry_ref[...] = upmeta_ref[...]

    @pl.when(i > 0)
    def _():
        uext_ref[0:POOL_HALO, :] = uext_ref[tm:tm + POOL_HALO, :]

    uext_ref[POOL_HALO:POOL_HALO + tm, :] = u_ref[0]
    o_pool = _pool_tile(uext_ref, tm, N_META + i * tm, pw_ref, ps_ref)
    xp = _mix_residual(x_ref[0], oa_ref[0], o_pool, wout_ref)
    h = _rmsnorm(xp, g2_ref[...]).astype(BF16)

    f = jnp.zeros_like(xp)
    for c in range(d_ff // FFN_CHUNK):
        halves = []
        for part in range(2):
            cols = slice(part * d_ff + c * FFN_CHUNK, part * d_ff + (c + 1) * FFN_CHUNK)
            up = _dot(h, wup_ref[:, cols])
            ext_ref[0:CONV_HALO, cols] = carry_ref[:, cols]
            ext_ref[CONV_HALO:CONV_HALO + tm, cols] = up
            carry_ref[:, cols] = up[tm - CONV_HALO:, :]
            conv_ref[0, :, cols] = ext_ref[pl.ds(CONV_HALO + tm - (CONV_W - 1), CONV_W - 1), cols]
            halves.append(_conv3(cw_ref, cb_ref, cols,
                                 ext_ref[pl.ds(CONV_HALO - 2, tm), cols],
                                 ext_ref[pl.ds(CONV_HALO - 1, tm), cols], up))
        gate, val = halves
        act = gate * jax.nn.sigmoid(gate) * val
        f = f + _dot(act.astype(BF16), wdown_ref[c * FFN_CHUNK:(c + 1) * FFN_CHUNK, :])
    y_ref[0] = _rmsnorm(xp + f, g3_ref[...])


def _mixffn_prompt(x, o_attn, u, u_meta, up_meta_tail, weights, tm):
    b, s, d = x.shape
    pool_w, pool_scale, w_out, g2, w_up, conv_w, conv_b, w_down, g3 = weights
    d_attn, d_pool, d_ff2 = o_attn.shape[-1], u.shape[-1], w_up.shape[1]
    const = lambda a: pl.BlockSpec(a.shape, lambda b, i: (0,) * a.ndim, pipeline_mode=pl.Buffered(1))
    row_spec = lambda width: pl.BlockSpec((1, tm, width), lambda b, i: (b, i, 0))
    return pl.pallas_call(
        _mixffn_kernel,
        grid=(b, s // tm),
        in_specs=[row_spec(d), row_spec(d_attn), row_spec(d_pool), const(u_meta), const(up_meta_tail),
                  const(pool_w), const(pool_scale), const(w_out), const(g2), const(w_up),
                  const(conv_w), const(conv_b), const(w_down), const(g3)],
        out_specs=[row_spec(d), pl.BlockSpec((1, CONV_W - 1, d_ff2), lambda b, i: (b, 0, 0))],
        out_shape=[jax.ShapeDtypeStruct((b, s, d), F32),
                   jax.ShapeDtypeStruct((b, CONV_W - 1, d_ff2), F32)],
        scratch_shapes=[pltpu.VMEM((POOL_HALO + tm, d_pool), F32),
                        pltpu.VMEM((CONV_HALO, d_ff2), F32),
                        pltpu.VMEM((CONV_HALO + tm, d_ff2), F32)],
        compiler_params=pltpu.CompilerParams(dimension_semantics=("arbitrary", "arbitrary"),
                                             vmem_limit_bytes=VMEM_LIMIT),
        name="mixffn_prompt",
    )(x, o_attn, u, u_meta, up_meta_tail, pool_w, pool_scale, w_out, g2, w_up, conv_w, conv_b, w_down, g3)


def _meta_kernel(bias_ref, x_ref, q_ref, kb_ref, vb_ref, u_ref, pw_ref, ps_ref, wout_ref, g2_ref,
                 wup_ref, up_ref, uext_ref):
    n, d_attn = q_ref.shape
    n_keys = kb_ref.shape[0]
    dh = LANES // 2
    lane = lax.broadcasted_iota(jnp.int32, (n, LANES), 1)
    row = lax.broadcasted_iota(jnp.int32, (n, n_keys), 0)
    col = lax.broadcasted_iota(jnp.int32, (n, n_keys), 1)
    causal = col < row
    tri = _strict_lower_ones(n_keys)
    pairs = []
    for j in range(d_attn // LANES):
        cols = slice(j * LANES, (j + 1) * LANES)
        q, kb, vb = q_ref[:, cols], kb_ref[:, cols], vb_ref[:, cols]
        out = None
        for p in range(2):
            in_head = (lane >= dh) if p else (lane < dh)
            qm = jnp.where(in_head, q, jnp.zeros_like(q))
            z = _dot_nt(qm, kb) + bias_ref[2 * j + p] * LOG2_E
            pv, _ = _stick_block(z, vb, causal, tri, None)
            out = pv if p == 0 else jnp.where(lane < dh, out, pv)
        pairs.append(out)
    o_attn = jnp.concatenate(pairs, axis=1).astype(BF16)

    uext_ref[0:POOL_HALO, :] = jnp.zeros((POOL_HALO, uext_ref.shape[1]), F32)
    uext_ref[POOL_HALO:POOL_HALO + n, :] = u_ref[...]
    o_pool = _pool_tile(uext_ref, n, 0, pw_ref, ps_ref)
    xp = _mix_residual(x_ref[...], o_attn, o_pool, wout_ref)
    h = _rmsnorm(xp, g2_ref[...]).astype(BF16)
    up_ref[...] = _dot(h, wup_ref[...])


def _meta_rows(sb_bias, x, q, kb, vb, u, weights):
    pool_w, pool_scale, w_out, g2, w_up = weights
    n = x.shape[0]
    vmem = pl.BlockSpec(memory_space=pltpu.VMEM)
    return pl.pallas_call(
        _meta_kernel,
        in_specs=[pl.BlockSpec(memory_space=pltpu.SMEM)] + [vmem] * 10,
        out_specs=vmem,
        out_shape=jax.ShapeDtypeStruct((n, w_up.shape[1]), F32),
        scratch_shapes=[pltpu.VMEM((POOL_HALO + n, u.shape[1]), F32)],
        compiler_params=pltpu.CompilerParams(vmem_limit_bytes=VMEM_LIMIT),
        name="meta_rows",
    )(sb_bias, x, q, kb, vb, u, pool_w, pool_scale, w_out, g2, w_up)


def _mixffn_sample_kernel(x_ref, oa_ref, u_ref, pstate_ref, cstate_ref, pw_ref, ps_ref, wout_ref,
                          g2_ref, wup_ref, cw_ref, cb_ref, wdown_ref, g3_ref, y_ref, up_ref, *, pos):
    n_state = pstate_ref.shape[0]
    ch = pw_ref.shape[-1]
    d_ff = wdown_ref.shape[0]
    sums, cur, inv = [], [], []
    for g, w in enumerate(POOL_WINDOWS):
        lanes = slice(g * ch, (g + 1) * ch)
        c = u_ref[:, lanes]
        s = c
        for back in range(1, w):
            s = s + pstate_ref[n_state - back, :, lanes]
        cur.append(c)
        sums.append(s)
        inv.append(1.0 / min(w, pos + 1))
    o_pool = _pool_heads(sums, cur, inv, pw_ref, ps_ref)
    xp = _mix_residual(x_ref[...], oa_ref[...], o_pool, wout_ref)
    h = _rmsnorm(xp, g2_ref[...]).astype(BF16)
    up = _dot(h, wup_ref[...])
    up_ref[...] = up
    c = _conv3(cw_ref, cb_ref, slice(None), cstate_ref[0], cstate_ref[1], up)
    gate, val = c[:, :d_ff], c[:, d_ff:]
    act = gate * jax.nn.sigmoid(gate) * val
    f = _dot(act.astype(BF16), wdown_ref[...])
    y_ref[...] = _rmsnorm(xp + f, g3_ref[...])


def _mixffn_sample(x, o_attn, u, pool_state_t, conv_state_t, weights, pos):
    pool_w, pool_scale, w_out, g2, w_up, conv_w, conv_b, w_down, g3 = weights
    n, d = x.shape
    vmem = pl.BlockSpec(memory_space=pltpu.VMEM)
    return pl.pallas_call(
        functools.partial(_mixffn_sample_kernel, pos=pos),
        in_specs=[vmem] * 14,
        out_specs=[vmem, vmem],
        out_shape=[jax.ShapeDtypeStruct((n, d), F32), jax.ShapeDtypeStruct((n, w_up.shape[1]), F32)],
        compiler_params=pltpu.CompilerParams(vmem_limit_bytes=VMEM_LIMIT),
        name="mixffn_sample",
    )(x, o_attn, u, pool_state_t, conv_state_t, pool_w, pool_scale, w_out, g2, w_up, conv_w, conv_b,
      w_down, g3)


def kernel(x_prompt, x_sample, cache_k, cache_v, state_pool, state_conv, page_table, meta_tokens,
           norm_mix_g, w_in, sb_bias, pool_w, pool_scale, w_out, norm_ffn_g, w_up, conv_w, conv_b,
           w_down, norm_final_g):
    b, s, d = x_prompt.shape
    db, n_new, _ = x_sample.shape
    assert n_new == 1, "the sample path handles one new token per sequence"
    dh = cache_k.shape[-1]
    d_attn = H_A * dh
    d_pool = state_pool.shape[-1]
    past_len = page_table.shape[1] * cache_k.shape[1]
    t = N_META + s

    g1 = norm_mix_g.reshape(1, d)
    g2 = norm_ffn_g.reshape(1, d)
    g3 = norm_final_g.reshape(1, d)
    w_in_b, w_out_b, w_up_b, w_down_b = (w.astype(BF16) for w in (w_in, w_out, w_up, w_down))
    pool_w_b = pool_w.astype(BF16)
    conv_b2 = conv_b.reshape(1, -1)
    sb_bias = sb_bias.astype(F32)
    mix_weights = (pool_w_b, pool_scale, w_out_b, g2, w_up_b)
    ffn_weights = mix_weights + (conv_w, conv_b2, w_down_b, g3)

    q_m, k_m, v_m, u_m, kb_m, vb_m = _proj(meta_tokens, g1, w_in_b, d_attn, N_META)
    pad_meta = lambda a: jnp.pad(a, ((0, LANES - N_META), (0, 0)))
    kb_m, vb_m = pad_meta(kb_m), pad_meta(vb_m)
    up_m = _meta_rows(sb_bias, meta_tokens, q_m, kb_m, vb_m, u_m, mix_weights)

    q_r, k_r, v_r, u_r, kb_r, vb_r = _proj(x_prompt.reshape(b * s, d), g1, w_in_b, d_attn, PROJ_ROWS)
    seq = lambda a: a.reshape(b, s, a.shape[-1])
    o_attn = _attn_prompt(sb_bias, seq(q_r), seq(kb_r), seq(vb_r), kb_m, vb_m,
                          ATTN_Q_TILE, ATTN_K_BLOCK)
    y_prompt, conv_prompt = _mixffn_prompt(x_prompt, o_attn, seq(u_r), u_m, up_m[N_META - CONV_HALO:],
                                           ffn_weights, FFN_ROWS)
    with_meta = lambda m, r: jnp.concatenate(
        [jnp.broadcast_to(m[None], (b,) + m.shape), seq(r)], axis=1).reshape(b, t, H_A, dh)
    k_prompt = with_meta(k_m, k_r)
    v_prompt = with_meta(v_m, v_r)
    pool_prompt = seq(u_r)[:, s - state_pool.shape[1]:]

    q_s, k_s, v_s, u_s, _, _ = _proj(x_sample.reshape(db * n_new, d), g1, w_in_b, d_attn, db * n_new)
    tok = lambda a: a.reshape(db, n_new, a.shape[-1])
    o_attn_s = _attn_sample(sb_bias, tok(q_s), tok(k_s), tok(v_s), cache_k, cache_v, page_table)
    y_s, up_s = _mixffn_sample(x_sample.reshape(db, d), o_attn_s.reshape(db, d_attn), u_s,
                               jnp.swapaxes(state_pool, 0, 1), jnp.swapaxes(state_conv, 0, 1),
                               ffn_weights, past_len)
    y_sample = y_s.reshape(db, n_new, d)
    k_sample = k_s.reshape(db, n_new, H_A, dh)
    v_sample = v_s.reshape(db, n_new, H_A, dh)
    pool_sample = jnp.concatenate([state_pool[:, 1:], u_s.reshape(db, n_new, d_pool)], axis=1)
    conv_sample = jnp.concatenate([state_conv[:, 1:], up_s[:, None, :]], axis=1)

    return (y_prompt, y_sample, k_prompt, v_prompt, pool_prompt, conv_prompt,
            k_sample, v_sample, pool_sample, conv_sample)
```

```python
import functools

import jax
import jax.numpy as jnp
from jax import lax
from jax.experimental import pallas as pl
from jax.experimental.pallas import tpu as pltpu

F32 = jnp.float32
BF16 = jnp.bfloat16

N_META = 16
H_A = 8
POOL_WINDOWS = (2, 4, 8, 16)
CONV_W = 3
EPS = 1e-6
LOG2_E = 1.4426950408889634

LANES = 128
SUBLANES = 8
VMEM_LIMIT = 56 * 1024 * 1024

POOL_HALO = 16
CONV_HALO = SUBLANES

PROJ_ROWS = 512
ATTN_Q_TILE = 512
ATTN_K_BLOCK = 256
FFN_ROWS = 256
FFN_CHUNK = 256
PAGES_PER_STEP = 16


def _rmsnorm(x, g):
    r = lax.rsqrt(jnp.mean(x * x, axis=-1, keepdims=True) + EPS)
    return x * r * g


def _dot(a, b):
    return jnp.dot(a, b, preferred_element_type=F32)


def _dot_nt(a, b):
    return lax.dot_general(a, b, (((1,), (1,)), ((), ())), preferred_element_type=F32)


def _strict_lower_ones(n):
    j = lax.broadcasted_iota(jnp.int32, (n, n), 0)
    s = lax.broadcasted_iota(jnp.int32, (n, n), 1)
    return (j > s).astype(BF16)


def _softplus_terms(z):
    softplus = jnp.maximum(z, 0.0) + jnp.log2(1.0 + jnp.exp2(-jnp.abs(z)))
    return softplus, z - softplus


def _stick_block(z, vb, mask, tri, later):
    softplus, log_beta = _softplus_terms(z)
    if mask is not None:
        softplus = jnp.where(mask, softplus, 0.0)
    suffix = _dot(softplus.astype(BF16), tri)
    a = jnp.exp2(log_beta - suffix)
    if mask is not None:
        a = jnp.where(mask, a, 0.0)
    pv = _dot(a.astype(BF16), vb)
    block_sum = jnp.sum(softplus, axis=1, keepdims=True)
    if later is None:
        return pv, block_sum
    return pv * jnp.exp2(-later), later + block_sum


def _proj_kernel(x_ref, g_ref, w_ref, q_ref, k_ref, v_ref, u_ref, kb_ref, vb_ref, *, scale):
    d_attn = q_ref.shape[-1]
    h = _rmsnorm(x_ref[...], g_ref[...]).astype(BF16)
    p = _dot(h, w_ref[...])
    k = p[:, d_attn:2 * d_attn]
    v = p[:, 2 * d_attn:3 * d_attn]
    q_ref[...] = (p[:, :d_attn] * scale).astype(BF16)
    k_ref[...] = k
    v_ref[...] = v
    u_ref[...] = p[:, 3 * d_attn:]
    kb_ref[...] = k.astype(BF16)
    vb_ref[...] = v.astype(BF16)


def _proj(x, g, w_in, d_attn, tm):
    rows, d = x.shape
    n = w_in.shape[1]
    d_pool = n - 3 * d_attn
    dh = d_attn // H_A
    row_spec = lambda width: pl.BlockSpec((tm, width), lambda i: (i, 0))
    return pl.pallas_call(
        functools.partial(_proj_kernel, scale=dh ** -0.5 * LOG2_E),
        grid=(rows // tm,),
        in_specs=[row_spec(d),
                  pl.BlockSpec((1, d), lambda i: (0, 0)),
                  pl.BlockSpec((d, n), lambda i: (0, 0))],
        out_specs=[row_spec(d_attn), row_spec(d_attn), row_spec(d_attn), row_spec(d_pool),
                   row_spec(d_attn), row_spec(d_attn)],
        out_shape=[jax.ShapeDtypeStruct((rows, d_attn), BF16),
                   jax.ShapeDtypeStruct((rows, d_attn), F32),
                   jax.ShapeDtypeStruct((rows, d_attn), F32),
                   jax.ShapeDtypeStruct((rows, d_pool), F32),
                   jax.ShapeDtypeStruct((rows, d_attn), BF16),
                   jax.ShapeDtypeStruct((rows, d_attn), BF16)],
        compiler_params=pltpu.CompilerParams(dimension_semantics=("arbitrary",),
                                             vmem_limit_bytes=VMEM_LIMIT),
        name="proj",
    )(x, g, w_in)


def _attn_kernel(bias_ref, q_ref, k_ref, v_ref, km_ref, vm_ref, tri_ref, o_ref, acc_ref, later_ref):
    j = pl.program_id(1)
    i = pl.program_id(2)
    tq = q_ref.shape[1]
    tk = tri_ref.shape[0]
    n_sub = tq // tk
    dh = LANES // 2
    q = q_ref[0]
    lane = lax.broadcasted_iota(jnp.int32, (tq, LANES), 1)
    row = lax.broadcasted_iota(jnp.int32, (tk, tk), 0)
    col = lax.broadcasted_iota(jnp.int32, (tk, tk), 1)
    diag_mask = col < row
    n_meta_pad = km_ref.shape[0]
    meta_mask = lax.broadcasted_iota(jnp.int32, (tq, n_meta_pad), 1) < N_META
    tri = tri_ref[...]
    tri_meta = tri_ref[0:n_meta_pad, 0:n_meta_pad]
    heads = (0, 1)
    bias = [bias_ref[2 * j + p] * LOG2_E for p in heads]
    qm = [jnp.where((lane >= dh) if p else (lane < dh), q, jnp.zeros_like(q)) for p in heads]

    def key_block(block):
        start = pl.multiple_of(block * tk, tk)
        return k_ref[0, pl.ds(start, tk), :], v_ref[0, pl.ds(start, tk), :]

    for r in range(n_sub):
        rows = slice(r * tk, (r + 1) * tk)
        for p in heads:
            acc, later = 0.0, None
            for c in range(r, -1, -1):
                kb, vb = key_block(i * n_sub + c)
                z = _dot_nt(qm[p][rows], kb) + bias[p]
                pv, later = _stick_block(z, vb, diag_mask if c == r else None, tri, later)
                acc = acc + pv
            acc_ref[p, rows, :] = acc
            later_ref[p, rows, :] = later

    def earlier_blocks(n, carry):
        later = [later_ref[p] for p in heads]
        acc = [0.0 for p in heads]
        for c in range(n_sub):
            kb, vb = key_block((i - n) * n_sub - 1 - c)
            for p in heads:
                z = _dot_nt(qm[p], kb) + bias[p]
                pv, later[p] = _stick_block(z, vb, None, tri, later[p])
                acc[p] = acc[p] + pv
        for p in heads:
            acc_ref[p] += acc[p]
            later_ref[p] = later[p]
        return carry

    lax.fori_loop(0, i, earlier_blocks, 0)

    res = []
    for p in heads:
        z = _dot_nt(qm[p], km_ref[...]) + bias[p]
        pv, _ = _stick_block(z, vm_ref[...], meta_mask, tri_meta, later_ref[p])
        res.append(acc_ref[p] + pv)
    o_ref[0] = jnp.where(lane < dh, res[0], res[1]).astype(o_ref.dtype)


def _attn_prompt(sb_bias, q, kb, vb, km, vm, tq, tk):
    b, s, d_attn = q.shape
    tri = _strict_lower_ones(tk)
    n_meta_pad = km.shape[0]
    return pl.pallas_call(
        _attn_kernel,
        grid=(b, d_attn // LANES, s // tq),
        in_specs=[pl.BlockSpec(memory_space=pltpu.SMEM),
                  pl.BlockSpec((1, tq, LANES), lambda b, j, i: (b, i, j)),
                  pl.BlockSpec((1, s, LANES), lambda b, j, i: (b, 0, j)),
                  pl.BlockSpec((1, s, LANES), lambda b, j, i: (b, 0, j)),
                  pl.BlockSpec((n_meta_pad, LANES), lambda b, j, i: (0, j)),
                  pl.BlockSpec((n_meta_pad, LANES), lambda b, j, i: (0, j)),
                  pl.BlockSpec((tk, tk), lambda b, j, i: (0, 0))],
        out_specs=pl.BlockSpec((1, tq, LANES), lambda b, j, i: (b, i, j)),
        out_shape=jax.ShapeDtypeStruct((b, s, d_attn), BF16),
        scratch_shapes=[pltpu.VMEM((2, tq, LANES), F32), pltpu.VMEM((2, tq, 1), F32)],
        compiler_params=pltpu.CompilerParams(
            dimension_semantics=("arbitrary", "arbitrary", "arbitrary"),
            vmem_limit_bytes=VMEM_LIMIT),
        name="attn_prompt",
    )(sb_bias, q, kb, vb, km, vm, tri)


def _attn_sample_kernel(pt_ref, q_ref, knew_ref, vnew_ref, bias_ref, tri_ref, *refs, past_len):
    del pt_ref
    npg = (len(refs) - 4) // 2
    k_refs, v_refs = refs[:npg], refs[npg:2 * npg]
    o_ref, acc_ref, new_ref, later_ref = refs[2 * npg:]
    g = pl.program_id(1)
    d_attn = q_ref.shape[-1]
    dh = d_attn // H_A
    n_new = knew_ref.shape[1]
    head = lax.broadcasted_iota(jnp.int32, (H_A, d_attn), 0)
    lane = lax.broadcasted_iota(jnp.int32, (H_A, d_attn), 1)
    own = (lane >= head * dh) & (lane < (head + 1) * dh)
    qmat_f32 = jnp.where(own, q_ref[0].astype(F32), 0.0)
    qmat = qmat_f32.astype(BF16)
    bias = bias_ref[...] * LOG2_E
    tri = tri_ref[...]

    @pl.when(g == 0)
    def _():
        as_dot_operand = lambda a: a.astype(BF16).astype(F32)
        k_pos = past_len + lax.broadcasted_iota(jnp.int32, (H_A, n_new), 1)
        q_pos = past_len + n_new - 1
        visible = k_pos < q_pos
        z = jnp.sum(qmat_f32 * as_dot_operand(knew_ref[0]), axis=1, keepdims=True) + bias
        softplus, log_beta = _softplus_terms(z)
        a = jnp.where(visible, jnp.exp2(log_beta), 0.0)
        av = jnp.where(own, as_dot_operand(a) * as_dot_operand(vnew_ref[0]), 0.0)
        new_ref[...] = jnp.sum(av, axis=0, keepdims=True)
        later_ref[...] = jnp.where(visible, softplus, 0.0)
        acc_ref[...] = jnp.zeros_like(acc_ref)

    order = tuple(reversed(range(npg)))
    z = jnp.concatenate([_dot(qmat, k_refs[r][0].astype(BF16)) + bias for r in order], axis=0)
    softplus, log_beta = _softplus_terms(z)
    hi = softplus.astype(BF16)
    lo = (softplus - hi.astype(F32)).astype(BF16)
    suffix = _dot(hi, tri) + _dot(lo, tri)
    page_sum = jnp.sum(softplus, axis=1, keepdims=True)
    later = later_ref[...]
    weights = []
    for m in range(npg):
        rows = slice(m * H_A, (m + 1) * H_A)
        weights.append(jnp.exp2(log_beta[rows] - suffix[rows] - later))
        later = later + page_sum[rows]
    later_ref[...] = later

    for h in range(H_A):
        rows = slice(h * dh, (h + 1) * dh)
        acc = acc_ref[rows, :]
        for m, r in enumerate(order):
            acc = acc + v_refs[r][0, rows, :] * weights[m][h:h + 1, :]
        acc_ref[rows, :] = acc

    @pl.when(g == pl.num_programs(1) - 1)
    def _():
        o = jnp.sum(acc_ref[...].T, axis=0, keepdims=True) + new_ref[...]
        o_ref[0] = o.astype(o_ref.dtype)


def _attn_sample(sb_bias, q, k_new, v_new, cache_k, cache_v, page_table):
    db, n_new, d_attn = q.shape
    n_pool, page = cache_k.shape[0], cache_k.shape[1]
    n_pages = page_table.shape[1]
    npg = PAGES_PER_STEP
    n_steps = n_pages // npg
    ck = cache_k.transpose(0, 2, 3, 1).reshape(n_pool, d_attn, page)
    cv = cache_v.transpose(0, 2, 3, 1).reshape(n_pool, d_attn, page)
    tri = _strict_lower_ones(page)
    bias = sb_bias.reshape(H_A, 1)

    def page_spec(r):
        return pl.BlockSpec((1, d_attn, page),
                            lambda b, g, pt: (pt[b, (n_steps - 1 - g) * npg + r], 0, 0))

    tok_spec = pl.BlockSpec((1, n_new, d_attn), lambda b, g, pt: (b, 0, 0))
    grid_spec = pltpu.PrefetchScalarGridSpec(
        num_scalar_prefetch=1,
        grid=(db, n_steps),
        in_specs=[tok_spec, tok_spec, tok_spec,
                  pl.BlockSpec((H_A, 1), lambda b, g, pt: (0, 0)),
                  pl.BlockSpec((page, page), lambda b, g, pt: (0, 0))]
                 + [page_spec(r) for r in range(npg)] + [page_spec(r) for r in range(npg)],
        out_specs=tok_spec,
        scratch_shapes=[pltpu.VMEM((d_attn, page), F32), pltpu.VMEM((n_new, d_attn), F32),
                        pltpu.VMEM((H_A, 1), F32)],
    )
    return pl.pallas_call(
        functools.partial(_attn_sample_kernel, past_len=n_pages * page),
        grid_spec=grid_spec,
        out_shape=jax.ShapeDtypeStruct((db, n_new, d_attn), BF16),
        compiler_params=pltpu.CompilerParams(dimension_semantics=("arbitrary", "arbitrary"),
                                             vmem_limit_bytes=VMEM_LIMIT),
        name="attn_sample",
    )(page_table, q, k_new, v_new, bias, tri, *([ck] * npg), *([cv] * npg))


def _pool_heads(window_sums, cur, inv_cnt, pw_ref, ps_ref):
    outs = []
    for g in range(len(POOL_WINDOWS)):
        d = window_sums[g] * inv_cnt[g] - cur[g]
        outs.append(_dot(d.astype(BF16), pw_ref[g]) * ps_ref[g:g + 1, :])
    return jnp.concatenate(outs, axis=1)


def _pool_tile(uext_ref, tm, pos0, pw_ref, ps_ref):
    ch = pw_ref.shape[-1]
    pos = pos0 + lax.broadcasted_iota(jnp.int32, (tm, ch), 0)
    sums, cur, inv = [], [], []
    for g, w in enumerate(POOL_WINDOWS):
        lanes = slice(g * ch, (g + 1) * ch)
        c = uext_ref[pl.ds(POOL_HALO, tm), lanes]
        s = c
        for back in range(1, w):
            s = s + uext_ref[pl.ds(POOL_HALO - back, tm), lanes]
        cur.append(c)
        sums.append(s)
        inv.append(1.0 / jnp.minimum(w, pos + 1).astype(F32))
    return _pool_heads(sums, cur, inv, pw_ref, ps_ref)


def _mix_residual(x, o_attn, o_pool, wout_ref):
    d_attn = o_attn.shape[-1]
    mix = _dot(o_attn, wout_ref[0:d_attn, :]) + _dot(o_pool.astype(BF16), wout_ref[d_attn:, :])
    return x + mix


def _conv3(cw_ref, cb_ref, cols, two_back, one_back, cur):
    return (cb_ref[:, cols] + cw_ref[0:1, cols] * two_back + cw_ref[1:2, cols] * one_back
            + cw_ref[2:3, cols] * cur)


def _mixffn_kernel(x_ref, oa_ref, u_ref, umeta_ref, upmeta_ref, pw_ref, ps_ref, wout_ref, g2_ref,
                   wup_ref, cw_ref, cb_ref, wdown_ref, g3_ref, y_ref, conv_ref,
                   uext_ref, carry_ref, ext_ref):
    i = pl.program_id(1)
    tm = x_ref.shape[1]
    d_ff = wdown_ref.shape[0]

    @pl.when(i == 0)
    def _():
        uext_ref[0:POOL_HALO, :] = umeta_ref[...]
        carry_ref[...] = upmeta_ref[...]

    @pl.when(i > 0)
    def _():
        uext_ref[0:POOL_HALO, :] = uext_ref[tm:tm + POOL_HALO, :]

    uext_ref[POOL_HALO:POOL_HALO + tm, :] = u_ref[0]
    o_pool = _pool_tile(uext_ref, tm, N_META + i * tm, pw_ref, ps_ref)
    xp = _mix_residual(x_ref[0], oa_ref[0], o_pool, wout_ref)
    h = _rmsnorm(xp, g2_ref[...]).astype(BF16)

    f = jnp.zeros_like(xp)
    for c in range(d_ff // FFN_CHUNK):
        halves = []
        for part in range(2):
            cols = slice(part * d_ff + c * FFN_CHUNK, part * d_ff + (c + 1) * FFN_CHUNK)
            up = _dot(h, wup_ref[:, cols])
            ext_ref[0:CONV_HALO, cols] = carry_ref[:, cols]
            ext_ref[CONV_HALO:CONV_HALO + tm, cols] = up
            carry_ref[:, cols] = up[tm - CONV_HALO:, :]
            conv_ref[0, :, cols] = ext_ref[pl.ds(CONV_HALO + tm - (CONV_W - 1), CONV_W - 1), cols]
            halves.append(_conv3(cw_ref, cb_ref, cols,
                                 ext_ref[pl.ds(CONV_HALO - 2, tm), cols],
                                 ext_ref[pl.ds(CONV_HALO - 1, tm), cols], up))
        gate, val = halves
        act = gate * jax.nn.sigmoid(gate) * val
        f = f + _dot(act.astype(BF16), wdown_ref[c * FFN_CHUNK:(c + 1) * FFN_CHUNK, :])
    y_ref[0] = _rmsnorm(xp + f, g3_ref[...])


def _mixffn_prompt(x, o_attn, u, u_meta, up_meta_tail, weights, tm):
    b, s, d = x.shape
    pool_w, pool_scale, w_out, g2, w_up, conv_w, conv_b, w_down, g3 = weights
    d_attn, d_pool, d_ff2 = o_attn.shape[-1], u.shape[-1], w_up.shape[1]
    const = lambda a: pl.BlockSpec(a.shape, lambda b, i: (0,) * a.ndim, pipeline_mode=pl.Buffered(1))
    row_spec = lambda width: pl.BlockSpec((1, tm, width), lambda b, i: (b, i, 0))
    return pl.pallas_call(
        _mixffn_kernel,
        grid=(b, s // tm),
        in_specs=[row_spec(d), row_spec(d_attn), row_spec(d_pool), const(u_meta), const(up_meta_tail),
                  const(pool_w), const(pool_scale), const(w_out), const(g2), const(w_up),
                  const(conv_w), const(conv_b), const(w_down), const(g3)],
        out_specs=[row_spec(d), pl.BlockSpec((1, CONV_W - 1, d_ff2), lambda b, i: (b, 0, 0))],
        out_shape=[jax.ShapeDtypeStruct((b, s, d), F32),
                   jax.ShapeDtypeStruct((b, CONV_W - 1, d_ff2), F32)],
        scratch_shapes=[pltpu.VMEM((POOL_HALO + tm, d_pool), F32),
                        pltpu.VMEM((CONV_HALO, d_ff2), F32),
                        pltpu.VMEM((CONV_HALO + tm, d_ff2), F32)],
        compiler_params=pltpu.CompilerParams(dimension_semantics=("arbitrary", "arbitrary"),
                                             vmem_limit_bytes=VMEM_LIMIT),
        name="mixffn_prompt",
    )(x, o_attn, u, u_meta, up_meta_tail, pool_w, pool_scale, w_out, g2, w_up, conv_w, conv_b, w_down, g3)


def _meta_kernel(bias_ref, x_ref, q_ref, kb_ref, vb_ref, u_ref, pw_ref, ps_ref, wout_ref, g2_ref,
                 wup_ref, up_ref, uext_ref):
    n, d_attn = q_ref.shape
    n_keys = kb_ref.shape[0]
    dh = LANES // 2
    lane = lax.broadcasted_iota(jnp.int32, (n, LANES), 1)
    row = lax.broadcasted_iota(jnp.int32, (n, n_keys), 0)
    col = lax.broadcasted_iota(jnp.int32, (n, n_keys), 1)
    causal = col < row
    tri = _strict_lower_ones(n_keys)
    pairs = []
    for j in range(d_attn // LANES):
        cols = slice(j * LANES, (j + 1) * LANES)
        q, kb, vb = q_ref[:, cols], kb_ref[:, cols], vb_ref[:, cols]
        out = None
        for p in range(2):
            in_head = (lane >= dh) if p else (lane < dh)
            qm = jnp.where(in_head, q, jnp.zeros_like(q))
            z = _dot_nt(qm, kb) + bias_ref[2 * j + p] * LOG2_E
            pv, _ = _stick_block(z, vb, causal, tri, None)
            out = pv if p == 0 else jnp.where(lane < dh, out, pv)
        pairs.append(out)
    o_attn = jnp.concatenate(pairs, axis=1).astype(BF16)

    uext_ref[0:POOL_HALO, :] = jnp.zeros((POOL_HALO, uext_ref.shape[1]), F32)
    uext_ref[POOL_HALO:POOL_HALO + n, :] = u_ref[...]
    o_pool = _pool_tile(uext_ref, n, 0, pw_ref, ps_ref)
    xp = _mix_residual(x_ref[...], o_attn, o_pool, wout_ref)
    h = _rmsnorm(xp, g2_ref[...]).astype(BF16)
    up_ref[...] = _dot(h, wup_ref[...])


def _meta_rows(sb_bias, x, q, kb, vb, u, weights):
    pool_w, pool_scale, w_out, g2, w_up = weights
    n = x.shape[0]
    vmem = pl.BlockSpec(memory_space=pltpu.VMEM)
    return pl.pallas_call(
        _meta_kernel,
        in_specs=[pl.BlockSpec(memory_space=pltpu.SMEM)] + [vmem] * 10,
        out_specs=vmem,
        out_shape=jax.ShapeDtypeStruct((n, w_up.shape[1]), F32),
        scratch_shapes=[pltpu.VMEM((POOL_HALO + n, u.shape[1]), F32)],
        compiler_params=pltpu.CompilerParams(vmem_limit_bytes=VMEM_LIMIT),
        name="meta_rows",
    )(sb_bias, x, q, kb, vb, u, pool_w, pool_scale, w_out, g2, w_up)


def _mixffn_sample_kernel(x_ref, oa_ref, u_ref, pstate_ref, cstate_ref, pw_ref, ps_ref, wout_ref,
                          g2_ref, wup_ref, cw_ref, cb_ref, wdown_ref, g3_ref, y_ref, up_ref, *, pos):
    n_state = pstate_ref.shape[0]
    ch = pw_ref.shape[-1]
    d_ff = wdown_ref.shape[0]
    sums, cur, inv = [], [], []
    for g, w in enumerate(POOL_WINDOWS):
        lanes = slice(g * ch, (g + 1) * ch)
        c = u_ref[:, lanes]
        s = c
        for back in range(1, w):
            s = s + pstate_ref[n_state - back, :, lanes]
        cur.append(c)
        sums.append(s)
        inv.append(1.0 / min(w, pos + 1))
    o_pool = _pool_heads(sums, cur, inv, pw_ref, ps_ref)
    xp = _mix_residual(x_ref[...], oa_ref[...], o_pool, wout_ref)
    h = _rmsnorm(xp, g2_ref[...]).astype(BF16)
    up = _dot(h, wup_ref[...])
    up_ref[...] = up
    c = _conv3(cw_ref, cb_ref, slice(None), cstate_ref[0], cstate_ref[1], up)
    gate, val = c[:, :d_ff], c[:, d_ff:]
    act = gate * jax.nn.sigmoid(gate) * val
    f = _dot(act.astype(BF16), wdown_ref[...])
    y_ref[...] = _rmsnorm(xp + f, g3_ref[...])


def _mixffn_sample(x, o_attn, u, pool_state_t, conv_state_t, weights, pos):
    pool_w, pool_scale, w_out, g2, w_up, conv_w, conv_b, w_down, g3 = weights
    n, d = x.shape
    vmem = pl.BlockSpec(memory_space=pltpu.VMEM)
    return pl.pallas_call(
        functools.partial(_mixffn_sample_kernel, pos=pos),
        in_specs=[vmem] * 14,
        out_specs=[vmem, vmem],
        out_shape=[jax.ShapeDtypeStruct((n, d), F32), jax.ShapeDtypeStruct((n, w_up.shape[1]), F32)],
        compiler_params=pltpu.CompilerParams(vmem_limit_bytes=VMEM_LIMIT),
        name="mixffn_sample",
    )(x, o_attn, u, pool_state_t, conv_state_t, pool_w, pool_scale, w_out, g2, w_up, conv_w, conv_b,
      w_down, g3)


def kernel(x_prompt, x_sample, cache_k, cache_v, state_pool, state_conv, page_table, meta_tokens,
           norm_mix_g, w_in, sb_bias, pool_w, pool_scale, w_out, norm_ffn_g, w_up, conv_w, conv_b,
           w_down, norm_final_g):
    b, s, d = x_prompt.shape
    db, n_new, _ = x_sample.shape
    assert n_new == 1, "the sample path handles one new token per sequence"
    dh = cache_k.shape[-1]
    d_attn = H_A * dh
    d_pool = state_pool.shape[-1]
    past_len = page_table.shape[1] * cache_k.shape[1]
    t = N_META + s

    g1 = norm_mix_g.reshape(1, d)
    g2 = norm_ffn_g.reshape(1, d)
    g3 = norm_final_g.reshape(1, d)
    w_in_b, w_out_b, w_up_b, w_down_b = (w.astype(BF16) for w in (w_in, w_out, w_up, w_down))
    pool_w_b = pool_w.astype(BF16)
    conv_b2 = conv_b.reshape(1, -1)
    sb_bias = sb_bias.astype(F32)
    mix_weights = (pool_w_b, pool_scale, w_out_b, g2, w_up_b)
    ffn_weights = mix_weights + (conv_w, conv_b2, w_down_b, g3)

    q_m, k_m, v_m, u_m, kb_m, vb_m = _proj(meta_tokens, g1, w_in_b, d_attn, N_META)
    pad_meta = lambda a: jnp.pad(a, ((0, LANES - N_META), (0, 0)))
    kb_m, vb_m = pad_meta(kb_m), pad_meta(vb_m)
    up_m = _meta_rows(sb_bias, meta_tokens, q_m, kb_m, vb_m, u_m, mix_weights)

    q_r, k_r, v_r, u_r, kb_r, vb_r = _proj(x_prompt.reshape(b * s, d), g1, w_in_b, d_attn, PROJ_ROWS)
    seq = lambda a: a.reshape(b, s, a.shape[-1])
    o_attn = _attn_prompt(sb_bias, seq(q_r), seq(kb_r), seq(vb_r), kb_m, vb_m,
                          ATTN_Q_TILE, ATTN_K_BLOCK)
    y_prompt, conv_prompt = _mixffn_prompt(x_prompt, o_attn, seq(u_r), u_m, up_m[N_META - CONV_HALO:],
                                           ffn_weights, FFN_ROWS)
    with_meta = lambda m, r: jnp.concatenate(
        [jnp.broadcast_to(m[None], (b,) + m.shape), seq(r)], axis=1).reshape(b, t, H_A, dh)
    k_prompt = with_meta(k_m, k_r)
    v_prompt = with_meta(v_m, v_r)
    pool_prompt = seq(u_r)[:, s - state_pool.shape[1]:]

    q_s, k_s, v_s, u_s, _, _ = _proj(x_sample.reshape(db * n_new, d), g1, w_in_b, d_attn, db * n_new)
    tok = lambda a: a.reshape(db, n_new, a.shape[-1])
    o_attn_s = _attn_sample(sb_bias, tok(q_s), tok(k_s), tok(v_s), cache_k, cache_v, page_table)
    y_s, up_s = _mixffn_sample(x_sample.reshape(db, d), o_attn_s.reshape(db, d_attn), u_s,
                               jnp.swapaxes(state_pool, 0, 1), jnp.swapaxes(state_conv, 0, 1),
                               ffn_weights, past_len)
    y_sample = y_s.reshape(db, n_new, d)
    k_sample = k_s.reshape(db, n_new, H_A, dh)
    v_sample = v_s.reshape(db, n_new, H_A, dh)
    pool_sample = jnp.concatenate([state_pool[:, 1:], u_s.reshape(db, n_new, d_pool)], axis=1)
    conv_sample = jnp.concatenate([state_conv[:, 1:], up_s[:, None, :]], axis=1)

    return (y_prompt, y_sample, k_prompt, v_prompt, pool_prompt, conv_prompt,
            k_sample, v_sample, pool_sample, conv_sample)
```

```python
import functools

import jax
import jax.numpy as jnp
from jax import lax
from jax.experimental import pallas as pl
from jax.experimental.pallas import tpu as pltpu

F32 = jnp.float32
BF16 = jnp.bfloat16

N_META = 16
H_A = 8
POOL_WINDOWS = (2, 4, 8, 16)
CONV_W = 3
EPS = 1e-6
LOG2_E = 1.4426950408889634

LANES = 128
SUBLANES = 8
VMEM_LIMIT = 56 * 1024 * 1024

POOL_HALO = 16
CONV_HALO = SUBLANES

PROJ_ROWS = 512
ATTN_Q_TILE = 512
ATTN_K_BLOCK = 256
FFN_ROWS = 512
FFN_CHUNK = 256
PAGES_PER_STEP = 32


def _rmsnorm(x, g):
    r = lax.rsqrt(jnp.mean(x * x, axis=-1, keepdims=True) + EPS)
    return x * r * g


def _dot(a, b):
    return jnp.dot(a, b, preferred_element_type=F32)


def _dot_nt(a, b):
    return lax.dot_general(a, b, (((1,), (1,)), ((), ())), preferred_element_type=F32)


def _strict_lower_ones(n):
    j = lax.broadcasted_iota(jnp.int32, (n, n), 0)
    s = lax.broadcasted_iota(jnp.int32, (n, n), 1)
    return (j > s).astype(BF16)


def _softplus_terms(z):
    softplus = jnp.maximum(z, 0.0) + jnp.log2(1.0 + jnp.exp2(-jnp.abs(z)))
    return softplus, z - softplus


def _stick_block(z, vb, mask, tri, later):
    softplus, log_beta = _softplus_terms(z)
    if mask is not None:
        softplus = jnp.where(mask, softplus, 0.0)
    suffix = _dot(softplus.astype(BF16), tri)
    a = jnp.exp2(log_beta - suffix)
    if mask is not None:
        a = jnp.where(mask, a, 0.0)
    pv = _dot(a.astype(BF16), vb)
    block_sum = jnp.sum(softplus, axis=1, keepdims=True)
    if later is None:
        return pv, block_sum
    return pv * jnp.exp2(-later), later + block_sum


def _proj_kernel(x_ref, g_ref, w_ref, q_ref, k_ref, v_ref, u_ref, kb_ref, vb_ref, *, scale):
    d_attn = q_ref.shape[-1]
    h = _rmsnorm(x_ref[...], g_ref[...]).astype(BF16)
    p = _dot(h, w_ref[...])
    k = p[:, d_attn:2 * d_attn]
    v = p[:, 2 * d_attn:3 * d_attn]
    q_ref[...] = (p[:, :d_attn] * scale).astype(BF16)
    k_ref[...] = k
    v_ref[...] = v
    u_ref[...] = p[:, 3 * d_attn:]
    kb_ref[...] = k.astype(BF16)
    vb_ref[...] = v.astype(BF16)


def _proj(x, g, w_in, d_attn, tm):
    rows, d = x.shape
    n = w_in.shape[1]
    d_pool = n - 3 * d_attn
    dh = d_attn // H_A
    row_spec = lambda width: pl.BlockSpec((tm, width), lambda i: (i, 0))
    return pl.pallas_call(
        functools.partial(_proj_kernel, scale=dh ** -0.5 * LOG2_E),
        grid=(rows // tm,),
        in_specs=[row_spec(d),
                  pl.BlockSpec((1, d), lambda i: (0, 0)),
                  pl.BlockSpec((d, n), lambda i: (0, 0))],
        out_specs=[row_spec(d_attn), row_spec(d_attn), row_spec(d_attn), row_spec(d_pool),
                   row_spec(d_attn), row_spec(d_attn)],
        out_shape=[jax.ShapeDtypeStruct((rows, d_attn), BF16),
                   jax.ShapeDtypeStruct((rows, d_attn), F32),
                   jax.ShapeDtypeStruct((rows, d_attn), F32),
                   jax.ShapeDtypeStruct((rows, d_pool), F32),
                   jax.ShapeDtypeStruct((rows, d_attn), BF16),
                   jax.ShapeDtypeStruct((rows, d_attn), BF16)],
        compiler_params=pltpu.CompilerParams(dimension_semantics=("arbitrary",),
                                             vmem_limit_bytes=VMEM_LIMIT),
        name="proj",
    )(x, g, w_in)


def _attn_kernel(bias_ref, q_ref, k_ref, v_ref, km_ref, vm_ref, tri_ref, o_ref, acc_ref, later_ref):
    j = pl.program_id(1)
    i = pl.program_id(2)
    tq = q_ref.shape[1]
    tk = tri_ref.shape[0]
    n_sub = tq // tk
    dh = LANES // 2
    q = q_ref[0]
    lane = lax.broadcasted_iota(jnp.int32, (tq, LANES), 1)
    row = lax.broadcasted_iota(jnp.int32, (tk, tk), 0)
    col = lax.broadcasted_iota(jnp.int32, (tk, tk), 1)
    diag_mask = col < row
    n_meta_pad = km_ref.shape[0]
    meta_mask = lax.broadcasted_iota(jnp.int32, (tq, n_meta_pad), 1) < N_META
    tri = tri_ref[...]
    tri_meta = tri_ref[0:n_meta_pad, 0:n_meta_pad]
    heads = (0, 1)
    bias = [bias_ref[2 * j + p] * LOG2_E for p in heads]
    qm = [jnp.where((lane >= dh) if p else (lane < dh), q, jnp.zeros_like(q)) for p in heads]

    def key_block(block):
        start = pl.multiple_of(block * tk, tk)
        return k_ref[0, pl.ds(start, tk), :], v_ref[0, pl.ds(start, tk), :]

    for r in range(n_sub):
        rows = slice(r * tk, (r + 1) * tk)
        for p in heads:
            acc, later = 0.0, None
            for c in range(r, -1, -1):
                kb, vb = key_block(i * n_sub + c)
                z = _dot_nt(qm[p][rows], kb) + bias[p]
                pv, later = _stick_block(z, vb, diag_mask if c == r else None, tri, later)
                acc = acc + pv
            acc_ref[p, rows, :] = acc
            later_ref[p, rows, :] = later

    def earlier_blocks(n, carry):
        later = [later_ref[p] for p in heads]
        acc = [0.0 for p in heads]
        for c in range(n_sub):
            kb, vb = key_block((i - n) * n_sub - 1 - c)
            for p in heads:
                z = _dot_nt(qm[p], kb) + bias[p]
                pv, later[p] = _stick_block(z, vb, None, tri, later[p])
                acc[p] = acc[p] + pv
        for p in heads:
            acc_ref[p] += acc[p]
            later_ref[p] = later[p]
        return carry

    lax.fori_loop(0, i, earlier_blocks, 0)

    res = []
    for p in heads:
        z = _dot_nt(qm[p], km_ref[...]) + bias[p]
        pv, _ = _stick_block(z, vm_ref[...], meta_mask, tri_meta, later_ref[p])
        res.append(acc_ref[p] + pv)
    o_ref[0] = jnp.where(lane < dh, res[0], res[1]).astype(o_ref.dtype)


def _attn_prompt(sb_bias, q, kb, vb, km, vm, tq, tk):
    b, s, d_attn = q.shape
    tri = _strict_lower_ones(tk)
    n_meta_pad = km.shape[0]
    return pl.pallas_call(
        _attn_kernel,
        grid=(b, d_attn // LANES, s // tq),
        in_specs=[pl.BlockSpec(memory_space=pltpu.SMEM),
                  pl.BlockSpec((1, tq, LANES), lambda b, j, i: (b, i, j)),
                  pl.BlockSpec((1, s, LANES), lambda b, j, i: (b, 0, j)),
                  pl.BlockSpec((1, s, LANES), lambda b, j, i: (b, 0, j)),
                  pl.BlockSpec((n_meta_pad, LANES), lambda b, j, i: (0, j)),
                  pl.BlockSpec((n_meta_pad, LANES), lambda b, j, i: (0, j)),
                  pl.BlockSpec((tk, tk), lambda b, j, i: (0, 0))],
        out_specs=pl.BlockSpec((1, tq, LANES), lambda b, j, i: (b, i, j)),
        out_shape=jax.ShapeDtypeStruct((b, s, d_attn), BF16),
        scratch_shapes=[pltpu.VMEM((2, tq, LANES), F32), pltpu.VMEM((2, tq, 1), F32)],
        compiler_params=pltpu.CompilerParams(
            dimension_semantics=("arbitrary", "arbitrary", "arbitrary"),
            vmem_limit_bytes=VMEM_LIMIT),
        name="attn_prompt",
    )(sb_bias, q, kb, vb, km, vm, tri)


def _attn_sample_kernel(pt_ref, q_ref, knew_ref, vnew_ref, bias_ref, tri_ref, *refs, past_len):
    del pt_ref
    npg = (len(refs) - 4) // 2
    k_refs, v_refs = refs[:npg], refs[npg:2 * npg]
    o_ref, acc_ref, new_ref, later_ref = refs[2 * npg:]
    g = pl.program_id(1)
    d_attn = q_ref.shape[-1]
    dh = d_attn // H_A
    n_new = knew_ref.shape[1]
    head = lax.broadcasted_iota(jnp.int32, (H_A, d_attn), 0)
    lane = lax.broadcasted_iota(jnp.int32, (H_A, d_attn), 1)
    own = (lane >= head * dh) & (lane < (head + 1) * dh)
    qmat_f32 = jnp.where(own, q_ref[0].astype(F32), 0.0)
    qmat = qmat_f32.astype(BF16)
    bias = bias_ref[...] * LOG2_E
    tri = tri_ref[...]

    @pl.when(g == 0)
    def _():
        as_dot_operand = lambda a: a.astype(BF16).astype(F32)
        k_pos = past_len + lax.broadcasted_iota(jnp.int32, (H_A, n_new), 1)
        q_pos = past_len + n_new - 1
        visible = k_pos < q_pos
        z = jnp.sum(qmat_f32 * as_dot_operand(knew_ref[0]), axis=1, keepdims=True) + bias
        softplus, log_beta = _softplus_terms(z)
        a = jnp.where(visible, jnp.exp2(log_beta), 0.0)
        av = jnp.where(own, as_dot_operand(a) * as_dot_operand(vnew_ref[0]), 0.0)
        new_ref[...] = jnp.sum(av, axis=0, keepdims=True)
        later_ref[...] = jnp.where(visible, softplus, 0.0)
        acc_ref[...] = jnp.zeros_like(acc_ref)

    order = tuple(reversed(range(npg)))
    z = jnp.concatenate([_dot(qmat, k_refs[r][0].astype(BF16)) + bias for r in order], axis=0)
    softplus, log_beta = _softplus_terms(z)
    hi = softplus.astype(BF16)
    lo = (softplus - hi.astype(F32)).astype(BF16)
    suffix = _dot(hi, tri) + _dot(lo, tri)
    page_sum = jnp.sum(softplus, axis=1, keepdims=True)
    later = later_ref[...]
    weights = []
    for m in range(npg):
        rows = slice(m * H_A, (m + 1) * H_A)
        weights.append(jnp.exp2(log_beta[rows] - suffix[rows] - later))
        later = later + page_sum[rows]
    later_ref[...] = later

    for h in range(H_A):
        rows = slice(h * dh, (h + 1) * dh)
        acc = acc_ref[rows, :]
        for m, r in enumerate(order):
            acc = acc + v_refs[r][0, rows, :] * weights[m][h:h + 1, :]
        acc_ref[rows, :] = acc

    @pl.when(g == pl.num_programs(1) - 1)
    def _():
        o = jnp.sum(acc_ref[...].T, axis=0, keepdims=True) + new_ref[...]
        o_ref[0] = o.astype(o_ref.dtype)


def _attn_sample(sb_bias, q, k_new, v_new, cache_k, cache_v, page_table):
    db, n_new, d_attn = q.shape
    n_pool, page = cache_k.shape[0], cache_k.shape[1]
    n_pages = page_table.shape[1]
    npg = PAGES_PER_STEP
    n_steps = n_pages // npg
    ck = cache_k.transpose(0, 2, 3, 1).reshape(n_pool, d_attn, page)
    cv = cache_v.transpose(0, 2, 3, 1).reshape(n_pool, d_attn, page)
    tri = _strict_lower_ones(page)
    bias = sb_bias.reshape(H_A, 1)

    def page_spec(r):
        return pl.BlockSpec((1, d_attn, page),
                            lambda b, g, pt: (pt[b, (n_steps - 1 - g) * npg + r], 0, 0))

    tok_spec = pl.BlockSpec((1, n_new, d_attn), lambda b, g, pt: (b, 0, 0))
    grid_spec = pltpu.PrefetchScalarGridSpec(
        num_scalar_prefetch=1,
        grid=(db, n_steps),
        in_specs=[tok_spec, tok_spec, tok_spec,
                  pl.BlockSpec((H_A, 1), lambda b, g, pt: (0, 0)),
                  pl.BlockSpec((page, page), lambda b, g, pt: (0, 0))]
                 + [page_spec(r) for r in range(npg)] + [page_spec(r) for r in range(npg)],
        out_specs=tok_spec,
        scratch_shapes=[pltpu.VMEM((d_attn, page), F32), pltpu.VMEM((n_new, d_attn), F32),
                        pltpu.VMEM((H_A, 1), F32)],
    )
    return pl.pallas_call(
        functools.partial(_attn_sample_kernel, past_len=n_pages * page),
        grid_spec=grid_spec,
        out_shape=jax.ShapeDtypeStruct((db, n_new, d_attn), BF16),
        compiler_params=pltpu.CompilerParams(dimension_semantics=("arbitrary", "arbitrary"),
                                             vmem_limit_bytes=VMEM_LIMIT),
        name="attn_sample",
    )(page_table, q, k_new, v_new, bias, tri, *([ck] * npg), *([cv] * npg))


def _pool_heads(window_sums, cur, inv_cnt, pw_ref, ps_ref):
    outs = []
    for g in range(len(POOL_WINDOWS)):
        d = window_sums[g] * inv_cnt[g] - cur[g]
        outs.append(_dot(d.astype(BF16), pw_ref[g]) * ps_ref[g:g + 1, :])
    return jnp.concatenate(outs, axis=1)


def _pool_tile(uext_ref, tm, pos0, pw_ref, ps_ref):
    ch = pw_ref.shape[-1]
    pos = pos0 + lax.broadcasted_iota(jnp.int32, (tm, ch), 0)
    sums, cur, inv = [], [], []
    for g, w in enumerate(POOL_WINDOWS):
        lanes = slice(g * ch, (g + 1) * ch)
        c = uext_ref[pl.ds(POOL_HALO, tm), lanes]
        s = c
        for back in range(1, w):
            s = s + uext_ref[pl.ds(POOL_HALO - back, tm), lanes]
        cur.append(c)
        sums.append(s)
        inv.append(1.0 / jnp.minimum(w, pos + 1).astype(F32))
    return _pool_heads(sums, cur, inv, pw_ref, ps_ref)


def _mix_residual(x, o_attn, o_pool, wout_ref):
    d_attn = o_attn.shape[-1]
    mix = _dot(o_attn, wout_ref[0:d_attn, :]) + _dot(o_pool.astype(BF16), wout_ref[d_attn:, :])
    return x + mix


def _conv3(cw_ref, cb_ref, cols, two_back, one_back, cur):
    return (cb_ref[:, cols] + cw_ref[0:1, cols] * two_back + cw_ref[1:2, cols] * one_back
            + cw_ref[2:3, cols] * cur)


def _mixffn_kernel(x_ref, oa_ref, u_ref, umeta_ref, upmeta_ref, pw_ref, ps_ref, wout_ref, g2_ref,
                   wup_ref, cw_ref, cb_ref, wdown_ref, g3_ref, y_ref, conv_ref,
                   uext_ref, carry_ref, ext_ref):
    i = pl.program_id(1)
    tm = x_ref.shape[1]
    d_ff = wdown_ref.shape[0]

    @pl.when(i == 0)
    def _():
        uext_ref[0:POOL_HALO, :] = umeta_ref[...]
        carry_ref[...] = upmeta_ref[...]

    @pl.when(i > 0)
    def _():
        uext_ref[0:POOL_HALO, :] = uext_ref[tm:tm + POOL_HALO, :]

    uext_ref[POOL_HALO:POOL_HALO + tm, :] = u_ref[0]
    o_pool = _pool_tile(uext_ref, tm, N_META + i * tm, pw_ref, ps_ref)
    xp = _mix_residual(x_ref[0], oa_ref[0], o_pool, wout_ref)
    h = _rmsnorm(xp, g2_ref[...]).astype(BF16)

    f = jnp.zeros_like(xp)
    for c in range(d_ff // FFN_CHUNK):
        halves = []
        for part in range(2):
            cols = slice(part * d_ff + c * FFN_CHUNK, part * d_ff + (c + 1) * FFN_CHUNK)
            up = _dot(h, wup_ref[:, cols])
            ext_ref[0:CONV_HALO, cols] = carry_ref[:, cols]
            ext_ref[CONV_HALO:CONV_HALO + tm, cols] = up
            carry_ref[:, cols] = up[tm - CONV_HALO:, :]
            conv_ref[0, :, cols] = ext_ref[pl.ds(CONV_HALO + tm - (CONV_W - 1), CONV_W - 1), cols]
            halves.append(_conv3(cw_ref, cb_ref, cols,
                                 ext_ref[pl.ds(CONV_HALO - 2, tm), cols],
                                 ext_ref[pl.ds(CONV_HALO - 1, tm), cols], up))
        gate, val = halves
        act = gate * jax.nn.sigmoid(gate) * val
        f = f + _dot(act.astype(BF16), wdown_ref[c * FFN_CHUNK:(c + 1) * FFN_CHUNK, :])
    y_ref[0] = _rmsnorm(xp + f, g3_ref[...])


def _mixffn_prompt(x, o_attn, u, u_meta, up_meta_tail, weights, tm):
    b, s, d = x.shape
    pool_w, pool_scale, w_out, g2, w_up, conv_w, conv_b, w_down, g3 = weights
    d_attn, d_pool, d_ff2 = o_attn.shape[-1], u.shape[-1], w_up.shape[1]
    const = lambda a: pl.BlockSpec(a.shape, lambda b, i: (0,) * a.ndim, pipeline_mode=pl.Buffered(1))
    row_spec = lambda width: pl.BlockSpec((1, tm, width), lambda b, i: (b, i, 0))
    return pl.pallas_call(
        _mixffn_kernel,
        grid=(b, s // tm),
        in_specs=[row_spec(d), row_spec(d_attn), row_spec(d_pool), const(u_meta), const(up_meta_tail),
                  const(pool_w), const(pool_scale), const(w_out), const(g2), const(w_up),
                  const(conv_w), const(conv_b), const(w_down), const(g3)],
        out_specs=[row_spec(d), pl.BlockSpec((1, CONV_W - 1, d_ff2), lambda b, i: (b, 0, 0))],
        out_shape=[jax.ShapeDtypeStruct((b, s, d), F32),
                   jax.ShapeDtypeStruct((b, CONV_W - 1, d_ff2), F32)],
        scratch_shapes=[pltpu.VMEM((POOL_HALO + tm, d_pool), F32),
                        pltpu.VMEM((CONV_HALO, d_ff2), F32),
                        pltpu.VMEM((CONV_HALO + tm, d_ff2), F32)],
        compiler_params=pltpu.CompilerParams(dimension_semantics=("arbitrary", "arbitrary"),
                                             vmem_limit_bytes=VMEM_LIMIT),
        name="mixffn_prompt",
    )(x, o_attn, u, u_meta, up_meta_tail, pool_w, pool_scale, w_out, g2, w_up, conv_w, conv_b, w_down, g3)


def _meta_kernel(bias_ref, x_ref, q_ref, kb_ref, vb_ref, u_ref, pw_ref, ps_ref, wout_ref, g2_ref,
                 wup_ref, up_ref, uext_ref):
    n, d_attn = q_ref.shape
    n_keys = kb_ref.shape[0]
    dh = LANES // 2
    lane = lax.broadcasted_iota(jnp.int32, (n, LANES), 1)
    row = lax.broadcasted_iota(jnp.int32, (n, n_keys), 0)
    col = lax.broadcasted_iota(jnp.int32, (n, n_keys), 1)
    causal = col < row
    tri = _strict_lower_ones(n_keys)
    pairs = []
    for j in range(d_attn // LANES):
        cols = slice(j * LANES, (j + 1) * LANES)
        q, kb, vb = q_ref[:, cols], kb_ref[:, cols], vb_ref[:, cols]
        out = None
        for p in range(2):
            in_head = (lane >= dh) if p else (lane < dh)
            qm = jnp.where(in_head, q, jnp.zeros_like(q))
            z = _dot_nt(qm, kb) + bias_ref[2 * j + p] * LOG2_E
            pv, _ = _stick_block(z, vb, causal, tri, None)
            out = pv if p == 0 else jnp.where(lane < dh, out, pv)
        pairs.append(out)
    o_attn = jnp.concatenate(pairs, axis=1).astype(BF16)

    uext_ref[0:POOL_HALO, :] = jnp.zeros((POOL_HALO, uext_ref.shape[1]), F32)
    uext_ref[POOL_HALO:POOL_HALO + n, :] = u_ref[...]
    o_pool = _pool_tile(uext_ref, n, 0, pw_ref, ps_ref)
    xp = _mix_residual(x_ref[...], o_attn, o_pool, wout_ref)
    h = _rmsnorm(xp, g2_ref[...]).astype(BF16)
    up_ref[...] = _dot(h, wup_ref[...])


def _meta_rows(sb_bias, x, q, kb, vb, u, weights):
    pool_w, pool_scale, w_out, g2, w_up = weights
    n = x.shape[0]
    vmem = pl.BlockSpec(memory_space=pltpu.VMEM)
    return pl.pallas_call(
        _meta_kernel,
        in_specs=[pl.BlockSpec(memory_space=pltpu.SMEM)] + [vmem] * 10,
        out_specs=vmem,
        out_shape=jax.ShapeDtypeStruct((n, w_up.shape[1]), F32),
        scratch_shapes=[pltpu.VMEM((POOL_HALO + n, u.shape[1]), F32)],
        compiler_params=pltpu.CompilerParams(vmem_limit_bytes=VMEM_LIMIT),
        name="meta_rows",
    )(sb_bias, x, q, kb, vb, u, pool_w, pool_scale, w_out, g2, w_up)


def _mixffn_sample_kernel(x_ref, oa_ref, u_ref, pstate_ref, cstate_ref, pw_ref, ps_ref, wout_ref,
                          g2_ref, wup_ref, cw_ref, cb_ref, wdown_ref, g3_ref, y_ref, up_ref, *, pos):
    n_state = pstate_ref.shape[0]
    ch = pw_ref.shape[-1]
    d_ff = wdown_ref.shape[0]
    sums, cur, inv = [], [], []
    for g, w in enumerate(POOL_WINDOWS):
        lanes = slice(g * ch, (g + 1) * ch)
        c = u_ref[:, lanes]
        s = c
        for back in range(1, w):
            s = s + pstate_ref[n_state - back, :, lanes]
        cur.append(c)
        sums.append(s)
        inv.append(1.0 / min(w, pos + 1))
    o_pool = _pool_heads(sums, cur, inv, pw_ref, ps_ref)
    xp = _mix_residual(x_ref[...], oa_ref[...], o_pool, wout_ref)
    h = _rmsnorm(xp, g2_ref[...]).astype(BF16)
    up = _dot(h, wup_ref[...])
    up_ref[...] = up
    c = _conv3(cw_ref, cb_ref, slice(None), cstate_ref[0], cstate_ref[1], up)
    gate, val = c[:, :d_ff], c[:, d_ff:]
    act = gate * jax.nn.sigmoid(gate) * val
    f = _dot(act.astype(BF16), wdown_ref[...])
    y_ref[...] = _rmsnorm(xp + f, g3_ref[...])


def _mixffn_sample(x, o_attn, u, pool_state_t, conv_state_t, weights, pos):
    pool_w, pool_scale, w_out, g2, w_up, conv_w, conv_b, w_down, g3 = weights
    n, d = x.shape
    vmem = pl.BlockSpec(memory_space=pltpu.VMEM)
    return pl.pallas_call(
        functools.partial(_mixffn_sample_kernel, pos=pos),
        in_specs=[vmem] * 14,
        out_specs=[vmem, vmem],
        out_shape=[jax.ShapeDtypeStruct((n, d), F32), jax.ShapeDtypeStruct((n, w_up.shape[1]), F32)],
        compiler_params=pltpu.CompilerParams(vmem_limit_bytes=VMEM_LIMIT),
        name="mixffn_sample",
    )(x, o_attn, u, pool_state_t, conv_state_t, pool_w, pool_scale, w_out, g2, w_up, conv_w, conv_b,
      w_down, g3)


def kernel(x_prompt, x_sample, cache_k, cache_v, state_pool, state_conv, page_table, meta_tokens,
           norm_mix_g, w_in, sb_bias, pool_w, pool_scale, w_out, norm_ffn_g, w_up, conv_w, conv_b,
           w_down, norm_final_g):
    b, s, d = x_prompt.shape
    db, n_new, _ = x_sample.shape
    assert n_new == 1, "the sample path handles one new token per sequence"
    dh = cache_k.shape[-1]
    d_attn = H_A * dh
    d_pool = state_pool.shape[-1]
    past_len = page_table.shape[1] * cache_k.shape[1]
    t = N_META + s

    g1 = norm_mix_g.reshape(1, d)
    g2 = norm_ffn_g.reshape(1, d)
    g3 = norm_final_g.reshape(1, d)
    w_in_b, w_out_b, w_up_b, w_down_b = (w.astype(BF16) for w in (w_in, w_out, w_up, w_down))
    pool_w_b = pool_w.astype(BF16)
    conv_b2 = conv_b.reshape(1, -1)
    sb_bias = sb_bias.astype(F32)
    mix_weights = (pool_w_b, pool_scale, w_out_b, g2, w_up_b)
    ffn_weights = mix_weights + (conv_w, conv_b2, w_down_b, g3)

    q_m, k_m, v_m, u_m, kb_m, vb_m = _proj(meta_tokens, g1, w_in_b, d_attn, N_META)
    pad_meta = lambda a: jnp.pad(a, ((0, LANES - N_META), (0, 0)))
    kb_m, vb_m = pad_meta(kb_m), pad_meta(vb_m)
    up_m = _meta_rows(sb_bias, meta_tokens, q_m, kb_m, vb_m, u_m, mix_weights)

    q_r, k_r, v_r, u_r, kb_r, vb_r = _proj(x_prompt.reshape(b * s, d), g1, w_in_b, d_attn, PROJ_ROWS)
    seq = lambda a: a.reshape(b, s, a.shape[-1])
    o_attn = _attn_prompt(sb_bias, seq(q_r), seq(kb_r), seq(vb_r), kb_m, vb_m,
                          ATTN_Q_TILE, ATTN_K_BLOCK)
    y_prompt, conv_prompt = _mixffn_prompt(x_prompt, o_attn, seq(u_r), u_m, up_m[N_META - CONV_HALO:],
                                           ffn_weights, FFN_ROWS)
    with_meta = lambda m, r: jnp.concatenate(
        [jnp.broadcast_to(m[None], (b,) + m.shape), seq(r)], axis=1).reshape(b, t, H_A, dh)
    k_prompt = with_meta(k_m, k_r)
    v_prompt = with_meta(v_m, v_r)
    pool_prompt = seq(u_r)[:, s - state_pool.shape[1]:]

    q_s, k_s, v_s, u_s, _, _ = _proj(x_sample.reshape(db * n_new, d), g1, w_in_b, d_attn, db * n_new)
    tok = lambda a: a.reshape(db, n_new, a.shape[-1])
    o_attn_s = _attn_sample(sb_bias, tok(q_s), tok(k_s), tok(v_s), cache_k, cache_v, page_table)
    y_s, up_s = _mixffn_sample(x_sample.reshape(db, d), o_attn_s.reshape(db, d_attn), u_s,
                               jnp.swapaxes(state_pool, 0, 1), jnp.swapaxes(state_conv, 0, 1),
                               ffn_weights, past_len)
    y_sample = y_s.reshape(db, n_new, d)
    k_sample = k_s.reshape(db, n_new, H_A, dh)
    v_sample = v_s.reshape(db, n_new, H_A, dh)
    pool_sample = jnp.concatenate([state_pool[:, 1:], u_s.reshape(db, n_new, d_pool)], axis=1)
    conv_sample = jnp.concatenate([state_conv[:, 1:], up_s[:, None, :]], axis=1)

    return (y_prompt, y_sample, k_prompt, v_prompt, pool_prompt, conv_prompt,
            k_sample, v_sample, pool_sample, conv_sample)
```
